```python
import jax, jax.numpy as jnp
from jax import lax
import numpy as np

D_MODEL = 2048
BATCH = 4
SEQ = 8192
DEPTH = 1

MEM_LEN = 256
D_CONV = 2048
CONV_WIDTH = 31
ML_HEADS = 4
D_ML = 2048
ML_HEAD_DIM = D_ML // ML_HEADS
ML_CHUNK = 64
QK_CONV_WIDTH = 4
XA_HEADS = 4
D_XA = 2048
XA_HEAD_DIM = D_XA // XA_HEADS
N_BRANCH = 3
EPS = 1e-6

IN_GROUPS = (
    ("glu_a", D_CONV), ("glu_b", D_CONV), ("z_conv", D_CONV),
    ("qk_ml", 2 * D_ML), ("v_ml", D_ML), ("o_ml", D_ML), ("z_ml", D_ML),
    ("if_ml", 2 * ML_HEADS),
    ("q_xa", D_XA), ("z_xa", D_XA),
    ("gates", N_BRANCH * D_MODEL),
)
N_IN = 3 * D_CONV + 5 * D_ML + 2 * ML_HEADS + 2 * D_XA + N_BRANCH * D_MODEL

kernel_name = "hybrid_conformer_mlstm_memxattn_gated"


def rms_norm(x, g):
    xf = x.astype(jnp.float32)
    y = xf * lax.rsqrt(jnp.mean(xf * xf, axis=-1, keepdims=True) + EPS)
    return (y * g.astype(jnp.float32)).astype(x.dtype)


def layer_norm(x, g, b):
    xf = x.astype(jnp.float32)
    mu = jnp.mean(xf, axis=-1, keepdims=True)
    var = jnp.mean(jnp.square(xf - mu), axis=-1, keepdims=True)
    y = (xf - mu) * lax.rsqrt(var + EPS)
    return (y * g.astype(jnp.float32) + b.astype(jnp.float32)).astype(x.dtype)


def causal_depthwise_conv(x, w):
    width, ch = w.shape
    return lax.conv_general_dilated(
        x, w[:, None, :].astype(x.dtype), window_strides=(1,),
        padding=((width - 1, 0),), dimension_numbers=("NWC", "WIO", "NWC"),
        feature_group_count=ch)


def in_cols(h, w_in, name):
    start = 0
    for n, width in IN_GROUPS:
        if n == name:
            return h @ w_in[:, start:start + width]
        start += width
    raise ValueError(name)


def mlstm_chunkwise(q, k, v, log_i, log_f):
    B, H, S, Dh = q.shape
    L = ML_CHUNK
    nc = S // L

    def to_chunks(a):
        return jnp.moveaxis(a.reshape(B, H, nc, L, *a.shape[3:]), 2, 0)

    xs = (to_chunks(q), to_chunks(k), to_chunks(v), to_chunks(log_i), to_chunks(log_f))
    causal = jnp.tril(jnp.ones((L, L), dtype=bool))

    def step(carry, chunk):
        C, n, m = carry
        qj, kj, vj, li, lf = chunk
        b = jnp.cumsum(lf, axis=-1)
        d = b[..., :, None] - b[..., None, :] + li[..., None, :]
        d = jnp.where(causal, d, -jnp.inf)
        inter = b + m[..., None]
        m_row = jnp.maximum(inter, jnp.max(d, axis=-1))
        w_intra = jnp.exp(d - m_row[..., None])
        w_inter = jnp.exp(inter - m_row)
        s = jnp.einsum("bhid,bhjd->bhij", qj, kj) * w_intra
        num = (jnp.einsum("bhij,bhje->bhie", s, vj)
               + w_inter[..., None] * jnp.einsum("bhid,bhde->bhie", qj, C))
        den = jnp.sum(s, axis=-1) + w_inter * jnp.einsum("bhid,bhd->bhi", qj, n)
        h = num / jnp.maximum(jnp.abs(den), jnp.exp(-m_row))[..., None]
        b_last = b[..., -1]
        g = b_last[..., None] - b + li
        m_new = jnp.maximum(b_last + m, jnp.max(g, axis=-1))
        decay = jnp.exp(b_last + m - m_new)
        wk = jnp.exp(g - m_new[..., None])
        C = decay[..., None, None] * C + jnp.einsum("bhj,bhjd,bhje->bhde", wk, kj, vj)
        n = decay[..., None] * n + jnp.einsum("bhj,bhjd->bhd", wk, kj)
        return (C, n, m_new), h

    init = (jnp.zeros((B, H, Dh, Dh), jnp.float32),
            jnp.zeros((B, H, Dh), jnp.float32),
            jnp.zeros((B, H), jnp.float32))
    _, hs = lax.scan(step, init, xs)
    return jnp.moveaxis(hs, 0, 2).reshape(B, H, S, Dh)


def setup_inputs(seed: int = 0) -> dict:
    key = jax.random.key(seed)
    ks = jax.random.split(key, 24)
    f32 = jnp.float32

    def nrm(k, shape, scale):
        return jax.random.normal(k, shape, f32) * scale

    def gain(k, n):
        return 1.0 + 0.02 * jax.random.normal(k, (n,), f32)

    b_i = 0.1 * jax.random.normal(ks[3], (ML_HEADS,), f32)
    b_f = jnp.linspace(3.0, 6.0, ML_HEADS, dtype=f32) + 0.1 * jax.random.normal(ks[4], (ML_HEADS,), f32)
    return {
        "x": nrm(ks[0], (BATCH, SEQ, D_MODEL), 1.0),
        "mem": nrm(ks[1], (BATCH, MEM_LEN, D_MODEL), 1.0),
        "g_pre": gain(ks[2], D_MODEL),
        "w_in": nrm(ks[5], (D_MODEL, N_IN), D_MODEL ** -0.5),
        "b_if": jnp.concatenate([b_i, b_f]),
        "w_qk_conv": nrm(ks[6], (QK_CONV_WIDTH, 2 * D_ML), QK_CONV_WIDTH ** -0.5),
        "w_dw": nrm(ks[7], (CONV_WIDTH, D_CONV), CONV_WIDTH ** -0.5),
        "b_dw": nrm(ks[8], (D_CONV,), 0.02),
        "g_ln": gain(ks[9], D_CONV),
        "b_ln": nrm(ks[10], (D_CONV,), 0.02),
        "w_conv_out": nrm(ks[11], (D_CONV, D_MODEL), D_CONV ** -0.5),
        "g_ml_head": gain(ks[12], D_ML),
        "w_ml_out": nrm(ks[13], (D_ML, D_MODEL), D_ML ** -0.5),
        "g_mem": gain(ks[14], D_MODEL),
        "w_mem_kv": nrm(ks[15], (D_MODEL, 2 * D_XA), D_MODEL ** -0.5),
        "w_xa_out": nrm(ks[16], (D_XA, D_MODEL), D_XA ** -0.5),
        "w_out": nrm(ks[17], (D_MODEL, D_MODEL), D_MODEL ** -0.5),
        "g_post": gain(ks[18], D_MODEL),
    }


def reference(x, mem, g_pre, w_in, b_if, w_qk_conv, w_dw, b_dw, g_ln, b_ln,
              w_conv_out, g_ml_head, w_ml_out, g_mem, w_mem_kv, w_xa_out,
              w_out, g_post):
    B, S, _ = x.shape
    f32 = jnp.float32
    for _layer in range(DEPTH):
        h = rms_norm(x, g_pre)

        u = in_cols(h, w_in, "glu_a") * jax.nn.sigmoid(in_cols(h, w_in, "glu_b"))
        u = causal_depthwise_conv(u, w_dw) + b_dw.astype(u.dtype)
        u = jax.nn.silu(layer_norm(u, g_ln, b_ln))
        y_conv = (u * jax.nn.silu(in_cols(h, w_in, "z_conv"))) @ w_conv_out

        qk = jax.nn.silu(causal_depthwise_conv(in_cols(h, w_in, "qk_ml"), w_qk_conv))
        q_ml, k_ml = jnp.split(qk, 2, axis=-1)
        v_ml = in_cols(h, w_in, "v_ml")

        def heads(t):
            return t.reshape(B, S, ML_HEADS, ML_HEAD_DIM).transpose(0, 2, 1, 3).astype(f32)

        gif = in_cols(h, w_in, "if_ml").astype(f32) + b_if.astype(f32)
        log_i = gif[..., :ML_HEADS].transpose(0, 2, 1)
        log_f = jax.nn.log_sigmoid(gif[..., ML_HEADS:]).transpose(0, 2, 1)
        hm = mlstm_chunkwise(heads(q_ml), heads(k_ml) * (ML_HEAD_DIM ** -0.5),
                             heads(v_ml), log_i, log_f)
        hm = hm.transpose(0, 2, 1, 3).reshape(B, S, D_ML)
        hm = jax.nn.sigmoid(in_cols(h, w_in, "o_ml").astype(f32)) * hm
        hm = hm.reshape(B, S, ML_HEADS, ML_HEAD_DIM)
        hm = hm * lax.rsqrt(jnp.mean(hm * hm, axis=-1, keepdims=True) + EPS)
        hm = (hm * g_ml_head.astype(f32).reshape(ML_HEADS, ML_HEAD_DIM)).reshape(B, S, D_ML)
        hm = hm.astype(x.dtype)
        y_ml = (hm * jax.nn.silu(in_cols(h, w_in, "z_ml"))) @ w_ml_out

        kv = rms_norm(mem, g_mem) @ w_mem_kv
        k_m, v_m = jnp.split(kv, 2, axis=-1)
        k_m = k_m.reshape(B, -1, XA_HEADS, XA_HEAD_DIM)
        v_m = v_m.reshape(B, -1, XA_HEADS, XA_HEAD_DIM)
        q_x = in_cols(h, w_in, "q_xa").reshape(B, S, XA_HEADS, XA_HEAD_DIM)
        scores = jnp.einsum("bshd,bmhd->bhsm", q_x, k_m).astype(f32) * (XA_HEAD_DIM ** -0.5)
        p = jax.nn.softmax(scores, axis=-1).astype(x.dtype)
        o_x = jnp.einsum("bhsm,bmhd->bshd", p, v_m).reshape(B, S, D_XA)
        y_xa = (o_x * jax.nn.silu(in_cols(h, w_in, "z_xa"))) @ w_xa_out

        g_c, g_m, g_x = jnp.split(jax.nn.sigmoid(in_cols(h, w_in, "gates")), N_BRANCH, axis=-1)
        merged = g_c * y_conv + g_m * y_ml.astype(x.dtype) + g_x * y_xa
        x = x + rms_norm(merged @ w_out, g_post)
    return x
```

```python
import functools

import jax
import jax.numpy as jnp
from jax import lax
from jax.experimental import pallas as pl
from jax.experimental.pallas import tpu as pltpu

F32 = jnp.float32
BF16 = jnp.bfloat16

D_MODEL = 2048
N_HEADS = 4
HEAD_DIM = D_MODEL // N_HEADS
CONV_WIDTH = 31
QK_CONV_WIDTH = 4
EPS = 1e-6
NEG_BIG = -1e30

LANES = 128
SUBLANES = 8
VMEM_LIMIT = 56 * 1024 * 1024

ACT_ID, ACT_SILU, ACT_SIGMOID = 0, 1, 2


def _sigmoid(y):
    return 1.0 / (1.0 + jnp.exp(-y))


def _silu(y):
    return y * _sigmoid(y)


def _params(*sem):
    return pltpu.CompilerParams(dimension_semantics=sem, vmem_limit_bytes=VMEM_LIMIT)


def _prenorm_kernel(x_ref, g_ref, wif_ref, bif_ref, h_ref, gif_ref):
    xf = x_ref[...]
    y = xf * lax.rsqrt(jnp.mean(xf * xf, axis=-1, keepdims=True) + EPS) * g_ref[...]
    h_ref[...] = y.astype(BF16)
    gif_ref[...] = jnp.dot(y, wif_ref[...], preferred_element_type=F32,
                           precision=lax.Precision.HIGHEST) + bif_ref[...]


def _prenorm(x2, g, wif, bif, tm):
    n, d = x2.shape
    return pl.pallas_call(
        _prenorm_kernel,
        out_shape=(jax.ShapeDtypeStruct((n, d), BF16), jax.ShapeDtypeStruct((n, LANES), F32)),
        grid=(n // tm,),
        in_specs=[pl.BlockSpec((tm, d), lambda i: (i, 0)),
                  pl.BlockSpec((1, d), lambda i: (0, 0)),
                  pl.BlockSpec((d, LANES), lambda i: (0, 0)),
                  pl.BlockSpec((1, LANES), lambda i: (0, 0))],
        out_specs=(pl.BlockSpec((tm, d), lambda i: (i, 0)),
                   pl.BlockSpec((tm, LANES), lambda i: (i, 0))),
        compiler_params=_params("parallel"),
        name="prenorm",
    )(x2, g, wif, bif)


def _memnorm_kernel(x_ref, g_ref, h_ref):
    xf = x_ref[...]
    y = xf * lax.rsqrt(jnp.mean(xf * xf, axis=-1, keepdims=True) + EPS) * g_ref[...]
    h_ref[...] = y.astype(BF16)


def _memnorm(x2, g, tm):
    n, d = x2.shape
    return pl.pallas_call(
        _memnorm_kernel,
        out_shape=jax.ShapeDtypeStruct((n, d), BF16),
        grid=(n // tm,),
        in_specs=[pl.BlockSpec((tm, d), lambda i: (i, 0)),
                  pl.BlockSpec((1, d), lambda i: (0, 0))],
        out_specs=pl.BlockSpec((tm, d), lambda i: (i, 0)),
        compiler_params=_params("parallel"),
        name="memnorm",
    )(x2, g)


def _proj_kernel(h_ref, w_ref, o_ref, *, tile_acts):
    y = jnp.dot(h_ref[...], w_ref[...], preferred_element_type=F32)
    j = pl.program_id(0)
    for act, fn in ((ACT_ID, lambda t: t), (ACT_SILU, _silu), (ACT_SIGMOID, _sigmoid)):
        tiles = [t for t, a in enumerate(tile_acts) if a == act]
        if not tiles:
            continue
        if len(tiles) == len(tile_acts):
            o_ref[...] = fn(y).astype(o_ref.dtype)
            continue
        cond = functools.reduce(jnp.logical_or, [j == t for t in tiles])

        @pl.when(cond)
        def _(fn=fn):
            o_ref[...] = fn(y).astype(o_ref.dtype)


def _proj(h, w, col0, group_acts, tm, tn):
    n, d = h.shape
    per_group = D_MODEL // tn
    tile_acts = tuple(a for a in group_acts for _ in range(per_group))
    ncol = len(tile_acts)
    off = col0 * per_group
    return pl.pallas_call(
        functools.partial(_proj_kernel, tile_acts=tile_acts),
        out_shape=jax.ShapeDtypeStruct((n, ncol * tn), BF16),
        grid=(ncol, n // tm),
        in_specs=[pl.BlockSpec((tm, d), lambda j, i: (i, 0)),
                  pl.BlockSpec((d, tn), lambda j, i: (0, j + off))],
        out_specs=pl.BlockSpec((tm, tn), lambda j, i: (i, j)),
        compiler_params=_params("parallel", "parallel"),
        name="proj",
    )(h, w)


def _glu_kernel(h_ref, wa_ref, wb_ref, o_ref):
    hh = h_ref[...]
    a = jnp.dot(hh, wa_ref[...], preferred_element_type=F32)
    b = jnp.dot(hh, wb_ref[...], preferred_element_type=F32)
    o_ref[...] = (a * _sigmoid(b)).astype(o_ref.dtype)


def _glu(h, w, tm, tn):
    n, d = h.shape
    ncol = D_MODEL // tn
    return pl.pallas_call(
        _glu_kernel,
        out_shape=jax.ShapeDtypeStruct((n, D_MODEL), BF16),
        grid=(ncol, n // tm),
        in_specs=[pl.BlockSpec((tm, d), lambda j, i: (i, 0)),
                  pl.BlockSpec((d, tn), lambda j, i: (0, j)),
                  pl.BlockSpec((d, tn), lambda j, i: (0, j + ncol))],
        out_specs=pl.BlockSpec((tm, tn), lambda j, i: (i, j)),
        compiler_params=_params("parallel", "parallel"),
        name="proj_glu",
    )(h, w, w)


CONV_HALO = 32
CONV_ROWS = 64
CONV_LANES = 256


def _conv_kernel(u_ref, prev_ref, z_ref, w_ref, bdw_ref, gln_ref, bln_ref, o_ref,
                 slab_ref, acc_ref, *, tm, tiles_per_seq):
    i = pl.program_id(0)
    first = (i % tiles_per_seq) == 0
    prev = prev_ref[...].astype(F32)
    slab_ref[0:CONV_HALO, :] = jnp.where(first, 0.0, prev)
    slab_ref[CONV_HALO:, :] = u_ref[...].astype(F32)

    base = CONV_HALO - (CONV_WIDTH - 1)
    span = CONV_ROWS + (base + CONV_WIDTH - 1) // SUBLANES * SUBLANES

    def lane_chunk(c, carry):
        l0 = pl.multiple_of(c * CONV_LANES, CONV_LANES)
        wv = w_ref[:, pl.ds(l0, CONV_LANES)]
        for r0 in range(0, tm, CONV_ROWS):
            acc = jnp.zeros((CONV_ROWS, CONV_LANES), F32)
            for b in range(SUBLANES):
                taps = [j for j in range(CONV_WIDTH) if (base + j) % SUBLANES == b]
                if not taps:
                    continue
                a_max = max((base + j) // SUBLANES for j in taps)
                rows = CONV_ROWS + a_max * SUBLANES
                assert rows <= span
                sb = slab_ref[pl.ds(r0 + b, rows), pl.ds(l0, CONV_LANES)]
                for j in taps:
                    a = (base + j) // SUBLANES
                    acc = acc + wv[j:j + 1, :] * sb[a * SUBLANES:a * SUBLANES + CONV_ROWS, :]
            acc_ref[r0:r0 + CONV_ROWS, pl.ds(l0, CONV_LANES)] = acc
        return carry

    lax.fori_loop(0, D_MODEL // CONV_LANES, lane_chunk, 0)

    y = acc_ref[...] + bdw_ref[...]
    mu = jnp.mean(y, axis=-1, keepdims=True)
    yc = y - mu
    var = jnp.mean(yc * yc, axis=-1, keepdims=True)
    yn = yc * lax.rsqrt(var + EPS) * gln_ref[...] + bln_ref[...]
    o_ref[...] = (_silu(yn) * z_ref[...].astype(F32)).astype(o_ref.dtype)


def _conv_branch(u, rest, z_col, w_dw_pad, b_dw, g_ln, b_ln, tm, seq):
    n, d = u.shape
    tiles_per_seq = seq // tm
    hb = tm // CONV_HALO
    return pl.pallas_call(
        functools.partial(_conv_kernel, tm=tm, tiles_per_seq=tiles_per_seq),
        out_shape=jax.ShapeDtypeStruct((n, d), BF16),
        grid=(n // tm,),
        in_specs=[pl.BlockSpec((tm, d), lambda i: (i, 0)),
                  pl.BlockSpec((CONV_HALO, d), lambda i: (jnp.maximum(i * hb - 1, 0), 0)),
                  pl.BlockSpec((tm, d), lambda i: (i, z_col)),
                  pl.BlockSpec((32, d), lambda i: (0, 0)),
                  pl.BlockSpec((1, d), lambda i: (0, 0)),
                  pl.BlockSpec((1, d), lambda i: (0, 0)),
                  pl.BlockSpec((1, d), lambda i: (0, 0))],
        out_specs=pl.BlockSpec((tm, d), lambda i: (i, 0)),
        scratch_shapes=[pltpu.VMEM((tm + CONV_HALO, d), F32), pltpu.VMEM((tm, d), F32)],
        compiler_params=_params("parallel"),
        name="conv_branch",
    )(u, u, rest, w_dw_pad, b_dw, g_ln, b_ln)


ML_CHUNK = 256
QK_HALO = 8


def _mlstm_kernel(q_ref, k_ref, v_ref, wq_ref, wk_ref, gc_ref, gr_ref, o_ref, z_ref, gh_ref,
                  out_ref, c_ref, n_ref, m_ref, qs_ref, ks_ref):
    hidx = pl.program_id(1)
    c = pl.program_id(2)
    L = ML_CHUNK

    @pl.when(c == 0)
    def _():
        c_ref[...] = jnp.zeros_like(c_ref)
        n_ref[...] = jnp.zeros_like(n_ref)
        m_ref[...] = jnp.zeros_like(m_ref)
        qs_ref[0:QK_HALO, :] = jnp.zeros((QK_HALO, HEAD_DIM), F32)
        ks_ref[0:QK_HALO, :] = jnp.zeros((QK_HALO, HEAD_DIM), F32)

    def conv4(pre_ref, slab_ref, w_ref):
        slab_ref[QK_HALO:, :] = pre_ref[...].astype(F32)
        acc = jnp.zeros((L, HEAD_DIM), F32)
        for j in range(QK_CONV_WIDTH):
            off = QK_HALO - (QK_CONV_WIDTH - 1) + j
            acc = acc + w_ref[j:j + 1, :] * slab_ref[pl.ds(off, L), :]
        slab_ref[0:QK_HALO, :] = slab_ref[L:L + QK_HALO, :]
        return _silu(acc)

    q = conv4(q_ref, qs_ref, wq_ref)
    k = conv4(k_ref, ks_ref, wk_ref) * (HEAD_DIM ** -0.5)
    v = v_ref[...]
    qb = q.astype(BF16)

    gcol = gc_ref[...]
    lane = lax.broadcasted_iota(jnp.int32, gcol.shape, 1)
    li_col = jnp.sum(jnp.where(lane == hidx, gcol, 0.0), axis=-1, keepdims=True)
    f_col = jnp.sum(jnp.where(lane == hidx + N_HEADS, gcol, 0.0), axis=-1, keepdims=True)
    grow = gr_ref[0]
    sub = lax.broadcasted_iota(jnp.int32, grow.shape, 0)
    li_row = jnp.sum(jnp.where(sub == hidx, grow, 0.0), axis=0, keepdims=True)
    f_row = jnp.sum(jnp.where(sub == hidx + N_HEADS, grow, 0.0), axis=0, keepdims=True)

    def log_sigmoid(t):
        return jnp.minimum(t, 0.0) - jnp.log(1.0 + jnp.exp(-jnp.abs(t)))

    lf_col = log_sigmoid(f_col)
    lf_row = log_sigmoid(f_row)

    ri = lax.broadcasted_iota(jnp.int32, (L, L), 0)
    ci = lax.broadcasted_iota(jnp.int32, (L, L), 1)
    causal = ri >= ci
    b_col = jnp.sum(jnp.where(causal, lf_row, 0.0), axis=-1, keepdims=True)
    b_row = jnp.sum(jnp.where(ri <= ci, lf_col, 0.0), axis=0, keepdims=True)

    m_prev = m_ref[0:1, 0:1]
    d = jnp.where(causal, b_col - b_row + li_row, NEG_BIG)
    inter = b_col + m_prev
    m_row = jnp.maximum(inter, jnp.max(d, axis=-1, keepdims=True))
    w_intra = jnp.exp(d - m_row)
    w_inter = jnp.exp(inter - m_row)

    s = lax.dot_general(qb, k.astype(BF16), (((1,), (1,)), ((), ())),
                        preferred_element_type=F32) * w_intra
    num = (jnp.dot(s.astype(BF16), v, preferred_element_type=F32)
           + w_inter * jnp.dot(qb, c_ref[...].astype(BF16), preferred_element_type=F32))
    den = (jnp.sum(s, axis=-1, keepdims=True)
           + w_inter * jnp.sum(q * n_ref[...], axis=-1, keepdims=True))
    hval = num / jnp.maximum(jnp.abs(den), jnp.exp(-m_row))

    b_last = b_col[L - 1:L, :]
    g_row = b_last - b_row + li_row
    g_col = b_last - b_col + li_col
    m_new = jnp.maximum(b_last + m_prev, jnp.max(g_row, axis=-1, keepdims=True))
    decay = jnp.exp(b_last + m_prev - m_new)
    kw = k * jnp.exp(g_col - m_new)
    c_ref[...] = decay * c_ref[...] + lax.dot_general(
        kw.astype(BF16), v, (((0,), (0,)), ((), ())), preferred_element_type=F32)
    n_ref[...] = decay * n_ref[...] + jnp.sum(kw, axis=0, keepdims=True)
    m_ref[...] = jnp.broadcast_to(m_new, m_ref.shape)

    hm = o_ref[...].astype(F32) * hval
    hm = hm * lax.rsqrt(jnp.mean(hm * hm, axis=-1, keepdims=True) + EPS) * gh_ref[...]
    out_ref[...] = (hm * z_ref[...].astype(F32)).astype(out_ref.dtype)


def _mlstm_branch(rest, cols, w_qk_conv, gif, gif_t, g_head, batch, seq):
    n = rest.shape[0]
    L = ML_CHUNK
    nc = seq // L
    hb = D_MODEL // HEAD_DIM

    def tok(col_group):
        return pl.BlockSpec((L, HEAD_DIM), lambda b, h, c: (b * nc + c, col_group * hb + h))

    return pl.pallas_call(
        _mlstm_kernel,
        out_shape=jax.ShapeDtypeStruct((n, D_MODEL), BF16),
        grid=(batch, N_HEADS, nc),
        in_specs=[tok(cols["q"]), tok(cols["k"]), tok(cols["v"]),
                  pl.BlockSpec((QK_CONV_WIDTH, HEAD_DIM), lambda b, h, c: (0, h)),
                  pl.BlockSpec((QK_CONV_WIDTH, HEAD_DIM), lambda b, h, c: (0, hb + h)),
                  pl.BlockSpec((L, LANES), lambda b, h, c: (b * nc + c, 0)),
                  pl.BlockSpec((1, SUBLANES, L), lambda b, h, c: (b, 0, c)),
                  tok(cols["o"]), tok(cols["z"]),
                  pl.BlockSpec((1, HEAD_DIM), lambda b, h, c: (0, h))],
        out_specs=pl.BlockSpec((L, HEAD_DIM), lambda b, h, c: (b * nc + c, h)),
        scratch_shapes=[pltpu.VMEM((HEAD_DIM, HEAD_DIM), F32),
                        pltpu.VMEM((1, HEAD_DIM), F32),
                        pltpu.VMEM((SUBLANES, LANES), F32),
                        pltpu.VMEM((L + QK_HALO, HEAD_DIM), F32),
                        pltpu.VMEM((L + QK_HALO, HEAD_DIM), F32)],
        compiler_params=_params("parallel", "parallel", "arbitrary"),
        name="mlstm",
    )(rest, rest, rest, w_qk_conv, w_qk_conv, gif, gif_t, rest, rest, g_head)


def _xattn_kernel(q_ref, k_ref, v_ref, z_ref, o_ref):
    scale = HEAD_DIM ** -0.5
    for h in range(N_HEADS):
        sl = slice(h * HEAD_DIM, (h + 1) * HEAD_DIM)
        s = lax.dot_general(q_ref[:, sl], k_ref[:, sl], (((1,), (1,)), ((), ())),
                            preferred_element_type=F32) * scale
        e = jnp.exp(s - jnp.max(s, axis=-1, keepdims=True))
        p = e / jnp.sum(e, axis=-1, keepdims=True)
        o = jnp.dot(p.astype(BF16), v_ref[:, sl], preferred_element_type=F32)
        o_ref[:, sl] = (o * z_ref[:, sl].astype(F32)).astype(o_ref.dtype)


def _xattn_branch(rest, q_col, z_col, kv, mem_len, tm, seq):
    n = rest.shape[0]
    tiles_per_seq = seq // tm
    return pl.pallas_call(
        _xattn_kernel,
        out_shape=jax.ShapeDtypeStruct((n, D_MODEL), BF16),
        grid=(n // tm,),
        in_specs=[pl.BlockSpec((tm, D_MODEL), lambda i: (i, q_col)),
                  pl.BlockSpec((mem_len, D_MODEL), lambda i: (i // tiles_per_seq, 0)),
                  pl.BlockSpec((mem_len, D_MODEL), lambda i: (i // tiles_per_seq, 1)),
                  pl.BlockSpec((tm, D_MODEL), lambda i: (i, z_col))],
        out_specs=pl.BlockSpec((tm, D_MODEL), lambda i: (i, 0)),
        compiler_params=_params("parallel"),
        name="xattn",
    )(rest, kv, kv, rest)


def _merge_kernel(ac_ref, am_ref, ax_ref, wc_ref, wm_ref, wx_ref, gc_ref, gm_ref, gx_ref, o_ref):
    acc = gc_ref[...].astype(F32) * jnp.dot(ac_ref[...], wc_ref[...], preferred_element_type=F32)
    acc = acc + gm_ref[...].astype(F32) * jnp.dot(am_ref[...], wm_ref[...], preferred_element_type=F32)
    acc = acc + gx_ref[...].astype(F32) * jnp.dot(ax_ref[...], wx_ref[...], preferred_element_type=F32)
    o_ref[...] = acc.astype(o_ref.dtype)


def _merge(a_c, a_m, a_x, w_c, w_m, w_x, rest, gate_col, tm, tn):
    n, d = a_c.shape
    ncol = D_MODEL // tn
    act = pl.BlockSpec((tm, d), lambda j, i: (i, 0))
    wsp = pl.BlockSpec((d, tn), lambda j, i: (0, j))

    def gate(k):
        return pl.BlockSpec((tm, tn), lambda j, i: (i, (gate_col + k) * ncol + j))

    return pl.pallas_call(
        _merge_kernel,
        out_shape=jax.ShapeDtypeStruct((n, D_MODEL), BF16),
        grid=(ncol, n // tm),
        in_specs=[act, act, act, wsp, wsp, wsp, gate(0), gate(1), gate(2)],
        out_specs=pl.BlockSpec((tm, tn), lambda j, i: (i, j)),
        compiler_params=_params("parallel", "parallel"),
        name="merge",
    )(a_c, a_m, a_x, w_c, w_m, w_x, rest, rest, rest)


def _final_kernel(m_ref, w_ref, g_ref, x_ref, o_ref):
    y = jnp.dot(m_ref[...], w_ref[...], preferred_element_type=F32)
    y = y * lax.rsqrt(jnp.mean(y * y, axis=-1, keepdims=True) + EPS) * g_ref[...]
    o_ref[...] = x_ref[...] + y


def _final(merged, w_out, g_post, x2, tm):
    n, d = x2.shape
    return pl.pallas_call(
        _final_kernel,
        out_shape=jax.ShapeDtypeStruct((n, d), F32),
        grid=(n // tm,),
        in_specs=[pl.BlockSpec((tm, d), lambda i: (i, 0)),
                  pl.BlockSpec((d, d), lambda i: (0, 0)),
                  pl.BlockSpec((1, d), lambda i: (0, 0)),
                  pl.BlockSpec((tm, d), lambda i: (i, 0))],
        out_specs=pl.BlockSpec((tm, d), lambda i: (i, 0)),
        compiler_params=_params("parallel"),
        name="final",
    )(merged, w_out, g_post, x2)


def kernel(x, mem, g_pre, w_in, b_if, w_qk_conv, w_dw, b_dw, g_ln, b_ln, w_conv_out, g_ml_head,
           w_ml_out, g_mem, w_mem_kv, w_xa_out, w_out, g_post):
    batch, seq, d = x.shape
    mem_len = mem.shape[1]
    n = batch * seq
    assert d == D_MODEL and seq % ML_CHUNK == 0

    if0 = 8 * D_MODEL
    nif = 2 * N_HEADS
    w_main = jnp.concatenate([w_in[:, :if0], w_in[:, if0 + nif:]], axis=1).astype(BF16)
    w_if = jnp.pad(w_in[:, if0:if0 + nif], ((0, 0), (0, LANES - nif)))
    b_if_pad = jnp.pad(b_if, (0, LANES - nif)).reshape(1, LANES)

    def row(vec):
        return vec.reshape(1, -1).astype(F32)

    x2 = x.reshape(n, d)
    h, gif = _prenorm(x2, row(g_pre), w_if, b_if_pad, tm=512)
    gif_t = gif[:, :SUBLANES].reshape(batch, seq, SUBLANES).transpose(0, 2, 1)

    u = _glu(h, w_main, tm=1024, tn=1024)
    rest_acts = (ACT_SILU, ACT_ID, ACT_ID, ACT_ID, ACT_SIGMOID, ACT_SILU, ACT_ID, ACT_SILU,
                 ACT_SIGMOID, ACT_SIGMOID, ACT_SIGMOID)
    rest = _proj(h, w_main, 2, rest_acts, tm=1024, tn=1024)
    col = dict(z_conv=0, q=1, k=2, v=3, o=4, z=5, q_xa=6, z_xa=7, gates=8)

    w_dw_pad = jnp.pad(w_dw, ((0, 32 - CONV_WIDTH), (0, 0)))
    a_c = _conv_branch(u, rest, col["z_conv"], w_dw_pad, row(b_dw), row(g_ln), row(b_ln),
                       tm=512, seq=seq)

    a_m = _mlstm_branch(rest, col, w_qk_conv, gif, gif_t, row(g_ml_head), batch, seq)

    mem_h = _memnorm(mem.reshape(batch * mem_len, d), row(g_mem), tm=256)
    kv = _proj(mem_h, w_mem_kv.astype(BF16), 0, (ACT_ID, ACT_ID), tm=256, tn=1024)
    a_x = _xattn_branch(rest, col["q_xa"], col["z_xa"], kv, mem_len, tm=512, seq=seq)

    merged = _merge(a_c, a_m, a_x, w_conv_out.astype(BF16), w_ml_out.astype(BF16),
                    w_xa_out.astype(BF16), rest, col["gates"], tm=512, tn=1024)
    out = _final(merged, w_out.astype(BF16), row(g_post), x2, tm=512)
    return out.reshape(batch, seq, d)
```

```python
import functools

import jax
import jax.numpy as jnp
from jax import lax
from jax.experimental import pallas as pl
from jax.experimental.pallas import tpu as pltpu

F32 = jnp.float32
BF16 = jnp.bfloat16

D_MODEL = 2048
N_HEADS = 4
HEAD_DIM = D_MODEL // N_HEADS
CONV_WIDTH = 31
QK_CONV_WIDTH = 4
EPS = 1e-6
NEG_BIG = -1e30

LANES = 128
SUBLANES = 8
VMEM_LIMIT = 56 * 1024 * 1024


def _sigmoid(y):
    return 1.0 / (1.0 + jnp.exp(-y))


def _silu(y):
    return y * _sigmoid(y)


def _identity(y):
    return y


def _params(*sem):
    return pltpu.CompilerParams(dimension_semantics=sem, vmem_limit_bytes=VMEM_LIMIT)


def _prenorm_kernel(x_ref, g_ref, whl_ref, bif_ref, h_ref, gif_ref):
    xf = x_ref[...]
    y = xf * lax.rsqrt(jnp.mean(xf * xf, axis=-1, keepdims=True) + EPS) * g_ref[...]
    hi = y.astype(BF16)
    h_ref[...] = hi
    lo = (y - hi.astype(F32)).astype(BF16)
    hh = jnp.dot(hi, whl_ref[...], preferred_element_type=F32)
    lh = jnp.dot(lo, whl_ref[:, 0:LANES], preferred_element_type=F32)
    gif_ref[...] = hh[:, 0:LANES] + (hh[:, LANES:] + lh) + bif_ref[...]


def _prenorm(x2, g, w_if_hl, bif, tm):
    n, d = x2.shape
    return pl.pallas_call(
        _prenorm_kernel,
        out_shape=(jax.ShapeDtypeStruct((n, d), BF16), jax.ShapeDtypeStruct((n, LANES), F32)),
        grid=(n // tm,),
        in_specs=[pl.BlockSpec((tm, d), lambda i: (i, 0)),
                  pl.BlockSpec((1, d), lambda i: (0, 0)),
                  pl.BlockSpec((d, 2 * LANES), lambda i: (0, 0)),
                  pl.BlockSpec((1, LANES), lambda i: (0, 0))],
        out_specs=(pl.BlockSpec((tm, d), lambda i: (i, 0)),
                   pl.BlockSpec((tm, LANES), lambda i: (i, 0))),
        compiler_params=_params("parallel"),
        name="prenorm",
    )(x2, g, w_if_hl, bif)


def _memnorm_kernel(x_ref, g_ref, h_ref):
    xf = x_ref[...]
    y = xf * lax.rsqrt(jnp.mean(xf * xf, axis=-1, keepdims=True) + EPS) * g_ref[...]
    h_ref[...] = y.astype(BF16)


def _memnorm(x2, g, tm):
    n, d = x2.shape
    return pl.pallas_call(
        _memnorm_kernel,
        out_shape=jax.ShapeDtypeStruct((n, d), BF16),
        grid=(n // tm,),
        in_specs=[pl.BlockSpec((tm, d), lambda i: (i, 0)),
                  pl.BlockSpec((1, d), lambda i: (0, 0))],
        out_specs=pl.BlockSpec((tm, d), lambda i: (i, 0)),
        compiler_params=_params("parallel"),
        name="memnorm",
    )(x2, g)


PROJ_SUB = 512


def _proj_kernel(h_ref, w_ref, o_ref, *, act):
    hh = h_ref[...]
    for c0 in range(0, o_ref.shape[1], PROJ_SUB):
        sl = slice(c0, c0 + PROJ_SUB)
        y = jnp.dot(hh, w_ref[:, sl], preferred_element_type=F32)
        o_ref[:, sl] = act(y).astype(o_ref.dtype)


def _proj(h, w, col0, ncols, act, tm, tn):
    n, d = h.shape
    off = col0 // tn
    return pl.pallas_call(
        functools.partial(_proj_kernel, act=act),
        out_shape=jax.ShapeDtypeStruct((n, ncols), BF16),
        grid=(ncols // tn, n // tm),
        in_specs=[pl.BlockSpec((tm, d), lambda j, i: (i, 0)),
                  pl.BlockSpec((d, tn), lambda j, i: (0, j + off))],
        out_specs=pl.BlockSpec((tm, tn), lambda j, i: (i, j)),
        compiler_params=_params("parallel", "parallel"),
        name="proj",
    )(h, w)


def _glu_kernel(h_ref, wa_ref, wb_ref, o_ref):
    hh = h_ref[...]
    a = jnp.dot(hh, wa_ref[...], preferred_element_type=F32)
    b = jnp.dot(hh, wb_ref[...], preferred_element_type=F32)
    o_ref[...] = (a * _sigmoid(b)).astype(o_ref.dtype)


def _glu(h, w, tm, tn):
    n, d = h.shape
    ncol = D_MODEL // tn
    return pl.pallas_call(
        _glu_kernel,
        out_shape=jax.ShapeDtypeStruct((n, D_MODEL), BF16),
        grid=(ncol, n // tm),
        in_specs=[pl.BlockSpec((tm, d), lambda j, i: (i, 0)),
                  pl.BlockSpec((d, tn), lambda j, i: (0, j)),
                  pl.BlockSpec((d, tn), lambda j, i: (0, j + ncol))],
        out_specs=pl.BlockSpec((tm, tn), lambda j, i: (i, j)),
        compiler_params=_params("parallel", "parallel"),
        name="proj_glu",
    )(h, w, w)


CONV_HALO = 32
CONV_KB = 8
CONV_RB = 16


def _conv_kernel(u_ref, prev_ref, z_ref, w_ref, bdw_ref, gln_ref, bln_ref, o_ref,
                 slab_ref, acc_ref, *, tm, tiles_per_seq):
    i = pl.program_id(0)
    first = (i % tiles_per_seq) == 0
    seg = tm // SUBLANES
    pitch = seg + 1
    base = CONV_HALO - (CONV_WIDTH - 1)
    assert CONV_HALO < seg and seg % CONV_HALO == 0

    def gap(nrow):
        return nrow + nrow // seg

    def lane_col(c, carry):
        l0 = pl.multiple_of(c * LANES, LANES)
        lanes = pl.ds(l0, LANES)
        slab_ref[0:CONV_HALO, :] = jnp.where(first, 0.0, prev_ref[:, lanes].astype(F32))
        slab_ref[CONV_HALO:seg, :] = u_ref[0:seg - CONV_HALO, lanes].astype(F32)
        for q in range(1, SUBLANES):
            slab_ref[q * pitch:q * pitch + seg, :] = (
                u_ref[q * seg - CONV_HALO:(q + 1) * seg - CONV_HALO, lanes].astype(F32))
        slab_ref[SUBLANES * pitch:SUBLANES * pitch + CONV_HALO, :] = (
            u_ref[tm - CONV_HALO:tm, lanes].astype(F32))
        for k0 in range(0, seg, CONV_KB):
            data = [slab_ref[pl.ds(gap(k0 + base + m), SUBLANES, stride=pitch), :]
                    for m in range(CONV_KB + CONV_WIDTH - 1)]
            accs = [jnp.zeros((SUBLANES, LANES), F32) for _ in range(CONV_KB)]
            for j in range(CONV_WIDTH):
                wj = w_ref[j, :, lanes]
                for kk in range(CONV_KB):
                    accs[kk] = accs[kk] + wj * data[kk + j]
            for kk in range(CONV_KB):
                acc_ref[c, pl.ds(k0 + kk, SUBLANES, stride=seg), :] = accs[kk]
        return carry

    lax.fori_loop(0, D_MODEL // LANES, lane_col, 0)

    def row_block(rb, carry):
        r0 = pl.multiple_of(rb * CONV_RB, CONV_RB)
        rows = pl.ds(r0, CONV_RB)
        y = acc_ref[:, rows, :] + bdw_ref[...]
        mu = jnp.sum(jnp.sum(y, axis=0), axis=-1, keepdims=True) * (1.0 / D_MODEL)
        yc = y - mu
        var = jnp.sum(jnp.sum(yc * yc, axis=0), axis=-1, keepdims=True) * (1.0 / D_MODEL)
        yn = yc * lax.rsqrt(var + EPS) * gln_ref[...] + bln_ref[...]
        for c in range(D_MODEL // LANES):
            sl = slice(c * LANES, (c + 1) * LANES)
            o_ref[rows, sl] = (_silu(yn[c]) * z_ref[rows, sl].astype(F32)).astype(o_ref.dtype)
        return carry

    lax.fori_loop(0, tm // CONV_RB, row_block, 0, unroll=8)


def _conv_branch(u, z, z_col, w_dw_b, b_dw, g_ln, b_ln, tm, seq):
    n, d = u.shape
    tiles_per_seq = seq // tm
    hb = tm // CONV_HALO
    vec = pl.BlockSpec((d // LANES, 1, LANES), lambda i: (0, 0, 0))
    return pl.pallas_call(
        functools.partial(_conv_kernel, tm=tm, tiles_per_seq=tiles_per_seq),
        out_shape=jax.ShapeDtypeStruct((n, d), BF16),
        grid=(n // tm,),
        in_specs=[pl.BlockSpec((tm, d), lambda i: (i, 0)),
                  pl.BlockSpec((CONV_HALO, d), lambda i: (jnp.maximum(i * hb - 1, 0), 0)),
                  pl.BlockSpec((tm, d), lambda i: (i, z_col)),
                  pl.BlockSpec((CONV_WIDTH, SUBLANES, d), lambda i: (0, 0, 0)),
                  vec, vec, vec],
        out_specs=pl.BlockSpec((tm, d), lambda i: (i, 0)),
        scratch_shapes=[pltpu.VMEM((tm + CONV_HALO + SUBLANES, LANES), F32),
                        pltpu.VMEM((d // LANES, tm, LANES), F32)],
        compiler_params=_params("parallel"),
        name="conv_branch",
    )(u, u, z, w_dw_b, b_dw, g_ln, b_ln)


ML_CHUNK = 256
QK_HALO = 8


def _mlstm_kernel(q_ref, k_ref, v_ref, wq_ref, wk_ref, gc_ref, gr_ref, o_ref, z_ref, gh_ref,
                  out_ref, c_ref, n_ref, m_ref, qs_ref, ks_ref):
    hidx = pl.program_id(1)
    c = pl.program_id(2)
    L = ML_CHUNK

    @pl.when(c == 0)
    def _():
        c_ref[...] = jnp.zeros_like(c_ref)
        n_ref[...] = jnp.zeros_like(n_ref)
        m_ref[...] = jnp.zeros_like(m_ref)
        qs_ref[0:QK_HALO, :] = jnp.zeros((QK_HALO, HEAD_DIM), F32)
        ks_ref[0:QK_HALO, :] = jnp.zeros((QK_HALO, HEAD_DIM), F32)

    def conv4(pre_ref, slab_ref, w_ref):
        slab_ref[QK_HALO:, :] = pre_ref[...].astype(F32)
        acc = jnp.zeros((L, HEAD_DIM), F32)
        for j in range(QK_CONV_WIDTH):
            off = QK_HALO - (QK_CONV_WIDTH - 1) + j
            acc = acc + w_ref[j:j + 1, :] * slab_ref[pl.ds(off, L), :]
        slab_ref[0:QK_HALO, :] = slab_ref[L:L + QK_HALO, :]
        return _silu(acc)

    q = conv4(q_ref, qs_ref, wq_ref)
    k = conv4(k_ref, ks_ref, wk_ref) * (HEAD_DIM ** -0.5)
    v = v_ref[...]
    qb = q.astype(BF16)

    gcol = gc_ref[...]
    lane = lax.broadcasted_iota(jnp.int32, gcol.shape, 1)
    li_col = jnp.sum(jnp.where(lane == hidx, gcol, 0.0), axis=-1, keepdims=True)
    f_col = jnp.sum(jnp.where(lane == hidx + N_HEADS, gcol, 0.0), axis=-1, keepdims=True)
    grow = gr_ref[0]
    sub = lax.broadcasted_iota(jnp.int32, grow.shape, 0)
    li_row = jnp.sum(jnp.where(sub == hidx, grow, 0.0), axis=0, keepdims=True)
    f_row = jnp.sum(jnp.where(sub == hidx + N_HEADS, grow, 0.0), axis=0, keepdims=True)

    def log_sigmoid(t):
        return jnp.minimum(t, 0.0) - jnp.log(1.0 + jnp.exp(-jnp.abs(t)))

    lf_col = log_sigmoid(f_col)
    lf_row = log_sigmoid(f_row)

    ri = lax.broadcasted_iota(jnp.int32, (L, L), 0)
    ci = lax.broadcasted_iota(jnp.int32, (L, L), 1)
    causal = ri >= ci
    b_col = jnp.sum(jnp.where(causal, lf_row, 0.0), axis=-1, keepdims=True)
    b_row = jnp.sum(jnp.where(ri <= ci, lf_col, 0.0), axis=0, keepdims=True)

    m_prev = m_ref[0:1, 0:1]
    d = jnp.where(causal, b_col - b_row + li_row, NEG_BIG)
    inter = b_col + m_prev
    m_row = jnp.maximum(inter, jnp.max(d, axis=-1, keepdims=True))
    w_intra = jnp.exp(d - m_row)
    w_inter = jnp.exp(inter - m_row)

    s = lax.dot_general(qb, k.astype(BF16), (((1,), (1,)), ((), ())),
                        preferred_element_type=F32) * w_intra
    num = (jnp.dot(s.astype(BF16), v, preferred_element_type=F32)
           + w_inter * jnp.dot(qb, c_ref[...].astype(BF16), preferred_element_type=F32))
    den = (jnp.sum(s, axis=-1, keepdims=True)
           + w_inter * jnp.sum(q * n_ref[...], axis=-1, keepdims=True))
    hval = num / jnp.maximum(jnp.abs(den), jnp.exp(-m_row))

    b_last = b_col[L - 1:L, :]
    g_row = b_last - b_row + li_row
    g_col = b_last - b_col + li_col
    m_new = jnp.maximum(b_last + m_prev, jnp.max(g_row, axis=-1, keepdims=True))
    decay = jnp.exp(b_last + m_prev - m_new)
    kw = k * jnp.exp(g_col - m_new)
    c_ref[...] = decay * c_ref[...] + lax.dot_general(
        kw.astype(BF16), v, (((0,), (0,)), ((), ())), preferred_element_type=F32)
    n_ref[...] = decay * n_ref[...] + jnp.sum(kw, axis=0, keepdims=True)
    m_ref[...] = jnp.broadcast_to(m_new, m_ref.shape)

    hm = o_ref[...].astype(F32) * hval
    hm = hm * lax.rsqrt(jnp.mean(hm * hm, axis=-1, keepdims=True) + EPS) * gh_ref[...]
    out_ref[...] = (hm * z_ref[...].astype(F32)).astype(out_ref.dtype)


def _mlstm_branch(ids, sg, zs, col, w_qk_conv, gif, gif_t, g_head, batch, seq):
    n = ids.shape[0]
    L = ML_CHUNK
    nc = seq // L
    hb = D_MODEL // HEAD_DIM

    def tok(col_group):
        return pl.BlockSpec((L, HEAD_DIM), lambda b, h, c: (b * nc + c, col_group * hb + h))

    return pl.pallas_call(
        _mlstm_kernel,
        out_shape=jax.ShapeDtypeStruct((n, D_MODEL), BF16),
        grid=(batch, N_HEADS, nc),
        in_specs=[tok(col["q"]), tok(col["k"]), tok(col["v"]),
                  pl.BlockSpec((QK_CONV_WIDTH, HEAD_DIM), lambda b, h, c: (0, h)),
                  pl.BlockSpec((QK_CONV_WIDTH, HEAD_DIM), lambda b, h, c: (0, hb + h)),
                  pl.BlockSpec((L, LANES), lambda b, h, c: (b * nc + c, 0)),
                  pl.BlockSpec((1, SUBLANES, L), lambda b, h, c: (b, 0, c)),
                  tok(col["o"]), tok(col["z_ml"]),
                  pl.BlockSpec((1, HEAD_DIM), lambda b, h, c: (0, h))],
        out_specs=pl.BlockSpec((L, HEAD_DIM), lambda b, h, c: (b * nc + c, h)),
        scratch_shapes=[pltpu.VMEM((HEAD_DIM, HEAD_DIM), F32),
                        pltpu.VMEM((1, HEAD_DIM), F32),
                        pltpu.VMEM((SUBLANES, LANES), F32),
                        pltpu.VMEM((L + QK_HALO, HEAD_DIM), F32),
                        pltpu.VMEM((L + QK_HALO, HEAD_DIM), F32)],
        compiler_params=_params("parallel", "parallel", "arbitrary"),
        name="mlstm",
    )(ids, ids, ids, w_qk_conv, w_qk_conv, gif, gif_t, sg, zs, g_head)


def _xattn_kernel(q_ref, k_ref, v_ref, z_ref, o_ref):
    scale = HEAD_DIM ** -0.5
    for h in range(N_HEADS):
        sl = slice(h * HEAD_DIM, (h + 1) * HEAD_DIM)
        s = lax.dot_general(q_ref[:, sl], k_ref[:, sl], (((1,), (1,)), ((), ())),
                            preferred_element_type=F32) * scale
        e = jnp.exp(s - jnp.max(s, axis=-1, keepdims=True))
        p = e / jnp.sum(e, axis=-1, keepdims=True)
        o = jnp.dot(p.astype(BF16), v_ref[:, sl], preferred_element_type=F32)
        o_ref[:, sl] = (o * z_ref[:, sl].astype(F32)).astype(o_ref.dtype)


def _xattn_branch(ids, q_col, zs, z_col, kv, mem_len, tm, seq):
    n = ids.shape[0]
    tiles_per_seq = seq // tm
    return pl.pallas_call(
        _xattn_kernel,
        out_shape=jax.ShapeDtypeStruct((n, D_MODEL), BF16),
        grid=(n // tm,),
        in_specs=[pl.BlockSpec((tm, D_MODEL), lambda i: (i, q_col)),
                  pl.BlockSpec((mem_len, D_MODEL), lambda i: (i // tiles_per_seq, 0)),
                  pl.BlockSpec((mem_len, D_MODEL), lambda i: (i // tiles_per_seq, 1)),
                  pl.BlockSpec((tm, D_MODEL), lambda i: (i, z_col))],
        out_specs=pl.BlockSpec((tm, D_MODEL), lambda i: (i, 0)),
        compiler_params=_params("parallel"),
        name="xattn",
    )(ids, kv, kv, zs)


def _merge_kernel(ac_ref, am_ref, ax_ref, wc_ref, wm_ref, wx_ref, gc_ref, gm_ref, gx_ref, o_ref):
    acc = gc_ref[...].astype(F32) * jnp.dot(ac_ref[...], wc_ref[...], preferred_element_type=F32)
    acc = acc + gm_ref[...].astype(F32) * jnp.dot(am_ref[...], wm_ref[...], preferred_element_type=F32)
    acc = acc + gx_ref[...].astype(F32) * jnp.dot(ax_ref[...], wx_ref[...], preferred_element_type=F32)
    o_ref[...] = acc.astype(o_ref.dtype)


def _merge(a_c, a_m, a_x, w_c, w_m, w_x, sg, gate_col, tm, tn):
    n, d = a_c.shape
    ncol = D_MODEL // tn
    act = pl.BlockSpec((tm, d), lambda j, i: (i, 0))
    wsp = pl.BlockSpec((d, tn), lambda j, i: (0, j))

    def gate(k):
        return pl.BlockSpec((tm, tn), lambda j, i: (i, (gate_col + k) * ncol + j))

    return pl.pallas_call(
        _merge_kernel,
        out_shape=jax.ShapeDtypeStruct((n, D_MODEL), BF16),
        grid=(ncol, n // tm),
        in_specs=[act, act, act, wsp, wsp, wsp, gate(0), gate(1), gate(2)],
        out_specs=pl.BlockSpec((tm, tn), lambda j, i: (i, j)),
        compiler_params=_params("parallel", "parallel"),
        name="merge",
    )(a_c, a_m, a_x, w_c, w_m, w_x, sg, sg, sg)


def _final_kernel(m_ref, w_ref, g_ref, x_ref, o_ref):
    y = jnp.dot(m_ref[...], w_ref[...], preferred_element_type=F32)
    y = y * lax.rsqrt(jnp.mean(y * y, axis=-1, keepdims=True) + EPS) * g_ref[...]
    o_ref[...] = x_ref[...] + y


def _final(merged, w_out, g_post, x2, tm):
    n, d = x2.shape
    return pl.pallas_call(
        _final_kernel,
        out_shape=jax.ShapeDtypeStruct((n, d), F32),
        grid=(n // tm,),
        in_specs=[pl.BlockSpec((tm, d), lambda i: (i, 0)),
                  pl.BlockSpec((d, d), lambda i: (0, 0)),
                  pl.BlockSpec((1, d), lambda i: (0, 0)),
                  pl.BlockSpec((tm, d), lambda i: (i, 0))],
        out_specs=pl.BlockSpec((tm, d), lambda i: (i, 0)),
        compiler_params=_params("parallel"),
        name="final",
    )(merged, w_out, g_post, x2)


def kernel(x, mem, g_pre, w_in, b_if, w_qk_conv, w_dw, b_dw, g_ln, b_ln, w_conv_out, g_ml_head,
           w_ml_out, g_mem, w_mem_kv, w_xa_out, w_out, g_post):
    batch, seq, d = x.shape
    mem_len = mem.shape[1]
    n = batch * seq
    assert d == D_MODEL and seq % ML_CHUNK == 0

    nif = 2 * N_HEADS

    def grp(g):
        start = g * D_MODEL + (nif if g >= 8 else 0)
        return w_in[:, start:start + D_MODEL]

    order = (0, 1, 2, 7, 9, 3, 4, 5, 8, 6, 10, 11, 12)
    w_main = jnp.concatenate([grp(g) for g in order], axis=1).astype(BF16)
    col = dict(z_conv=0, z_ml=1, z_xa=2, q=0, k=1, v=2, q_xa=3, o=0, gates=1)

    w_if = jnp.pad(w_in[:, 8 * D_MODEL:8 * D_MODEL + nif], ((0, 0), (0, LANES - nif)))
    w_if_hi = w_if.astype(BF16)
    w_if_lo = (w_if - w_if_hi.astype(F32)).astype(BF16)
    w_if_hl = jnp.concatenate([w_if_hi, w_if_lo], axis=1)
    b_if_pad = jnp.pad(b_if, (0, LANES - nif)).reshape(1, LANES)

    def row(vec):
        return vec.reshape(1, -1).astype(F32)

    def lane_cols(vec):
        return vec.reshape(d // LANES, 1, LANES).astype(F32)

    x2 = x.reshape(n, d)
    h, gif = _prenorm(x2, row(g_pre), w_if_hl, b_if_pad, tm=512)
    gif_t = gif[:, :SUBLANES].reshape(batch, seq, SUBLANES).transpose(0, 2, 1)

    u = _glu(h, w_main, tm=1024, tn=1024)
    zs = _proj(h, w_main, 2 * D_MODEL, 3 * D_MODEL, _silu, tm=1024, tn=2048)
    ids = _proj(h, w_main, 5 * D_MODEL, 4 * D_MODEL, _identity, tm=1024, tn=2048)
    sg = _proj(h, w_main, 9 * D_MODEL, 4 * D_MODEL, _sigmoid, tm=1024, tn=2048)

    w_dw_b = jnp.broadcast_to(w_dw.astype(F32)[:, None, :], (CONV_WIDTH, SUBLANES, d))
    a_c = _conv_branch(u, zs, col["z_conv"], w_dw_b, lane_cols(b_dw), lane_cols(g_ln),
                       lane_cols(b_ln), tm=512, seq=seq)

    a_m = _mlstm_branch(ids, sg, zs, col, w_qk_conv, gif, gif_t, row(g_ml_head), batch, seq)

    mem_h = _memnorm(mem.reshape(batch * mem_len, d), row(g_mem), tm=256)
    kv = _proj(mem_h, w_mem_kv.astype(BF16), 0, 2 * D_MODEL, _identity, tm=256, tn=2048)
    a_x = _xattn_branch(ids, col["q_xa"], zs, col["z_xa"], kv, mem_len, tm=512, seq=seq)

    merged = _merge(a_c, a_m, a_x, w_conv_out.astype(BF16), w_ml_out.astype(BF16),
                    w_xa_out.astype(BF16), sg, col["gates"], tm=512, tn=1024)
    out = _final(merged, w_out.astype(BF16), row(g_post), x2, tm=512)
    return out.reshape(batch, seq, d)
```

```python
import functools

import jax
import jax.numpy as jnp
from jax import lax
from jax.experimental import pallas as pl
from jax.experimental.pallas import tpu as pltpu

F32 = jnp.float32
BF16 = jnp.bfloat16

D_MODEL = 2048
N_HEADS = 4
HEAD_DIM = D_MODEL // N_HEADS
CONV_WIDTH = 31
QK_CONV_WIDTH = 4
EPS = 1e-6
NEG_BIG = -1e30

LANES = 128
SUBLANES = 8
VMEM_LIMIT = 56 * 1024 * 1024


def _sigmoid(y):
    return 1.0 / (1.0 + jnp.exp(-y))


def _silu(y):
    return y * _sigmoid(y)


def _identity(y):
    return y


def _params(*sem):
    return pltpu.CompilerParams(dimension_semantics=sem, vmem_limit_bytes=VMEM_LIMIT)


def _prenorm_kernel(x_ref, g_ref, whl_ref, bif_ref, h_ref, gif_ref):
    xf = x_ref[...]
    y = xf * lax.rsqrt(jnp.mean(xf * xf, axis=-1, keepdims=True) + EPS) * g_ref[...]
    hi = y.astype(BF16)
    h_ref[...] = hi
    lo = (y - hi.astype(F32)).astype(BF16)
    hh = jnp.dot(hi, whl_ref[...], preferred_element_type=F32)
    lh = jnp.dot(lo, whl_ref[:, 0:LANES], preferred_element_type=F32)
    gif_ref[...] = hh[:, 0:LANES] + (hh[:, LANES:] + lh) + bif_ref[...]


def _prenorm(x2, g, w_if_hl, bif, tm):
    n, d = x2.shape
    return pl.pallas_call(
        _prenorm_kernel,
        out_shape=(jax.ShapeDtypeStruct((n, d), BF16), jax.ShapeDtypeStruct((n, LANES), F32)),
        grid=(n // tm,),
        in_specs=[pl.BlockSpec((tm, d), lambda i: (i, 0)),
                  pl.BlockSpec((1, d), lambda i: (0, 0)),
                  pl.BlockSpec((d, 2 * LANES), lambda i: (0, 0)),
                  pl.BlockSpec((1, LANES), lambda i: (0, 0))],
        out_specs=(pl.BlockSpec((tm, d), lambda i: (i, 0)),
                   pl.BlockSpec((tm, LANES), lambda i: (i, 0))),
        compiler_params=_params("parallel"),
        name="prenorm",
    )(x2, g, w_if_hl, bif)


def _memnorm_kernel(x_ref, g_ref, h_ref):
    xf = x_ref[...]
    y = xf * lax.rsqrt(jnp.mean(xf * xf, axis=-1, keepdims=True) + EPS) * g_ref[...]
    h_ref[...] = y.astype(BF16)


def _memnorm(x2, g, tm):
    n, d = x2.shape
    return pl.pallas_call(
        _memnorm_kernel,
        out_shape=jax.ShapeDtypeStruct((n, d), BF16),
        grid=(n // tm,),
        in_specs=[pl.BlockSpec((tm, d), lambda i: (i, 0)),
                  pl.BlockSpec((1, d), lambda i: (0, 0))],
        out_specs=pl.BlockSpec((tm, d), lambda i: (i, 0)),
        compiler_params=_params("parallel"),
        name="memnorm",
    )(x2, g)


PROJ_SUB = 512


def _dot_t(a, w_t):
    return lax.dot_general(a, w_t, (((1,), (1,)), ((), ())), preferred_element_type=F32)


def _proj_kernel(h_ref, wt_ref, o_ref, *, act):
    hh = h_ref[...]
    for c0 in range(0, o_ref.shape[1], PROJ_SUB):
        sl = slice(c0, c0 + PROJ_SUB)
        o_ref[:, sl] = act(_dot_t(hh, wt_ref[sl, :])).astype(o_ref.dtype)


def _proj(h, w_t, row0, ncols, act, tm, tn):
    n, d = h.shape
    off = row0 // tn
    return pl.pallas_call(
        functools.partial(_proj_kernel, act=act),
        out_shape=jax.ShapeDtypeStruct((n, ncols), BF16),
        grid=(ncols // tn, n // tm),
        in_specs=[pl.BlockSpec((tm, d), lambda j, i: (i, 0)),
                  pl.BlockSpec((tn, d), lambda j, i: (j + off, 0))],
        out_specs=pl.BlockSpec((tm, tn), lambda j, i: (i, j)),
        compiler_params=_params("parallel", "parallel"),
        name="proj",
    )(h, w_t)


def _glu_kernel(h_ref, wa_ref, wb_ref, o_ref):
    hh = h_ref[...]
    a = _dot_t(hh, wa_ref[...])
    b = _dot_t(hh, wb_ref[...])
    o_ref[...] = (a * _sigmoid(b)).astype(o_ref.dtype)


def _glu(h, w_t, tm, tn):
    n, d = h.shape
    ncol = D_MODEL // tn
    return pl.pallas_call(
        _glu_kernel,
        out_shape=jax.ShapeDtypeStruct((n, D_MODEL), BF16),
        grid=(ncol, n // tm),
        in_specs=[pl.BlockSpec((tm, d), lambda j, i: (i, 0)),
                  pl.BlockSpec((tn, d), lambda j, i: (j, 0)),
                  pl.BlockSpec((tn, d), lambda j, i: (j + ncol, 0))],
        out_specs=pl.BlockSpec((tm, tn), lambda j, i: (i, j)),
        compiler_params=_params("parallel", "parallel"),
        name="proj_glu",
    )(h, w_t, w_t)


CONV_HALO = 32
CONV_KB = 8
CONV_RB = 16


def _conv_kernel(u_ref, prev_ref, z_ref, w_ref, bdw_ref, gln_ref, bln_ref, o_ref,
                 slab_ref, acc_ref, *, tm, tiles_per_seq):
    i = pl.program_id(0)
    first = (i % tiles_per_seq) == 0
    seg = tm // SUBLANES
    pitch = seg + 1
    base = CONV_HALO - (CONV_WIDTH - 1)
    assert CONV_HALO < seg and seg % CONV_HALO == 0

    def gap(nrow):
        return nrow + nrow // seg

    def lane_col(c, carry):
        l0 = pl.multiple_of(c * LANES, LANES)
        lanes = pl.ds(l0, LANES)
        slab_ref[0:CONV_HALO, :] = jnp.where(first, 0.0, prev_ref[:, lanes].astype(F32))
        slab_ref[CONV_HALO:seg, :] = u_ref[0:seg - CONV_HALO, lanes].astype(F32)
        for q in range(1, SUBLANES):
            slab_ref[q * pitch:q * pitch + seg, :] = (
                u_ref[q * seg - CONV_HALO:(q + 1) * seg - CONV_HALO, lanes].astype(F32))
        slab_ref[SUBLANES * pitch:SUBLANES * pitch + CONV_HALO, :] = (
            u_ref[tm - CONV_HALO:tm, lanes].astype(F32))
        for k0 in range(0, seg, CONV_KB):
            data = [slab_ref[pl.ds(gap(k0 + base + m), SUBLANES, stride=pitch), :]
                    for m in range(CONV_KB + CONV_WIDTH - 1)]
            accs = [jnp.zeros((SUBLANES, LANES), F32) for _ in range(CONV_KB)]
            for j in range(CONV_WIDTH):
                wj = w_ref[j, :, lanes]
                for kk in range(CONV_KB):
                    accs[kk] = accs[kk] + wj * data[kk + j]
            for kk in range(CONV_KB):
                acc_ref[c, pl.ds(k0 + kk, SUBLANES, stride=seg), :] = accs[kk]
        return carry

    lax.fori_loop(0, D_MODEL // LANES, lane_col, 0)

    def row_block(rb, carry):
        r0 = pl.multiple_of(rb * CONV_RB, CONV_RB)
        rows = pl.ds(r0, CONV_RB)
        y = acc_ref[:, rows, :] + bdw_ref[...]
        mu = jnp.sum(jnp.sum(y, axis=0), axis=-1, keepdims=True) * (1.0 / D_MODEL)
        yc = y - mu
        var = jnp.sum(jnp.sum(yc * yc, axis=0), axis=-1, keepdims=True) * (1.0 / D_MODEL)
        yn = yc * lax.rsqrt(var + EPS) * gln_ref[...] + bln_ref[...]
        for c in range(D_MODEL // LANES):
            sl = slice(c * LANES, (c + 1) * LANES)
            o_ref[rows, sl] = (_silu(yn[c]) * z_ref[rows, sl].astype(F32)).astype(o_ref.dtype)
        return carry

    lax.fori_loop(0, tm // CONV_RB, row_block, 0, unroll=8)


def _conv_branch(u, z, z_col, w_dw_b, b_dw, g_ln, b_ln, tm, seq):
    n, d = u.shape
    tiles_per_seq = seq // tm
    hb = tm // CONV_HALO
    vec = pl.BlockSpec((d // LANES, 1, LANES), lambda i: (0, 0, 0))
    return pl.pallas_call(
        functools.partial(_conv_kernel, tm=tm, tiles_per_seq=tiles_per_seq),
        out_shape=jax.ShapeDtypeStruct((n, d), BF16),
        grid=(n // tm,),
        in_specs=[pl.BlockSpec((tm, d), lambda i: (i, 0)),
                  pl.BlockSpec((CONV_HALO, d), lambda i: (jnp.maximum(i * hb - 1, 0), 0)),
                  pl.BlockSpec((tm, d), lambda i: (i, z_col)),
                  pl.BlockSpec((CONV_WIDTH, SUBLANES, d), lambda i: (0, 0, 0)),
                  vec, vec, vec],
        out_specs=pl.BlockSpec((tm, d), lambda i: (i, 0)),
        scratch_shapes=[pltpu.VMEM((tm + CONV_HALO + SUBLANES, LANES), F32),
                        pltpu.VMEM((d // LANES, tm, LANES), F32)],
        compiler_params=_params("parallel"),
        name="conv_branch",
    )(u, u, z, w_dw_b, b_dw, g_ln, b_ln)


ML_CHUNK = 256
QK_HALO = 16


def _log_sigmoid(t):
    return jnp.minimum(t, 0.0) - jnp.log(1.0 + jnp.exp(-jnp.abs(t)))


def _mlstm_kernel(q_ref, k_ref, qprev_ref, kprev_ref, v_ref, cw_ref, perm_ref, gc_ref, gr_ref,
                  o_ref, z_ref, gh_ref, out_ref, c_ref, n_ref, m_ref, slab_ref, qp_ref, kp_ref):
    first = pl.program_id(1) == 0
    L = ML_CHUNK
    seg = L // SUBLANES
    pitch = seg + 1
    base = QK_HALO - (QK_CONV_WIDTH - 1)
    assert QK_HALO < seg and seg % QK_HALO == 0

    @pl.when(first)
    def _():
        c_ref[...] = jnp.zeros_like(c_ref)
        n_ref[...] = jnp.zeros_like(n_ref)
        m_ref[...] = jnp.zeros_like(m_ref)

    def gap(nrow):
        return nrow + nrow // seg

    def conv_col(lc, carry):
        l0 = pl.multiple_of(lc * LANES, LANES)
        lanes = pl.ds(l0, LANES)
        for a, (pre_ref, prev_ref, dst_ref, scale) in enumerate((
                (q_ref, qprev_ref, qp_ref, 1.0), (k_ref, kprev_ref, kp_ref, HEAD_DIM ** -0.5))):
            slab_ref[a, 0:QK_HALO, :] = jnp.where(first, 0.0, prev_ref[:, lanes].astype(F32))
            slab_ref[a, QK_HALO:seg, :] = pre_ref[0:seg - QK_HALO, lanes].astype(F32)
            for s in range(1, SUBLANES):
                slab_ref[a, s * pitch:s * pitch + seg, :] = (
                    pre_ref[s * seg - QK_HALO:(s + 1) * seg - QK_HALO, lanes].astype(F32))
            slab_ref[a, SUBLANES * pitch:SUBLANES * pitch + QK_HALO, :] = (
                pre_ref[L - QK_HALO:L, lanes].astype(F32))
            data = [slab_ref[a, pl.ds(gap(base + m), SUBLANES, stride=pitch), :]
                    for m in range(seg + QK_CONV_WIDTH - 1)]
            wl = pl.ds(pl.multiple_of(a * D_MODEL + l0, LANES), LANES)
            w = [cw_ref[j, :, wl] for j in range(QK_CONV_WIDTH)]
            outs = []
            for kk in range(seg):
                acc = w[0] * data[kk]
                for j in range(1, QK_CONV_WIDTH):
                    acc = acc + w[j] * data[kk + j]
                outs.append(_silu(acc) * scale)
            for kk in range(0, seg, 2):
                dst_ref[kk * SUBLANES:(kk + 2) * SUBLANES, lanes] = (
                    jnp.concatenate(outs[kk:kk + 2], axis=0).astype(BF16))
        return carry

    lax.fori_loop(0, D_MODEL // LANES, conv_col, 0)

    perm = perm_ref[...]
    ri = lax.broadcasted_iota(jnp.int32, (L, L), 0)
    ci = lax.broadcasted_iota(jnp.int32, (L, L), 1)
    causal = ri >= ci
    upper = ri <= ci
    gcol = gc_ref[...]
    lane = lax.broadcasted_iota(jnp.int32, gcol.shape, 1)
    grow = gr_ref[0]
    sub = lax.broadcasted_iota(jnp.int32, grow.shape, 0)

    for h in range(N_HEADS):
        hs = slice(h * HEAD_DIM, (h + 1) * HEAD_DIM)
        q = jnp.dot(perm, qp_ref[:, hs], preferred_element_type=F32)
        k = jnp.dot(perm, kp_ref[:, hs], preferred_element_type=F32)
        v = v_ref[:, hs]
        qb = q.astype(BF16)

        li_col = jnp.sum(jnp.where(lane == h, gcol, 0.0), axis=-1, keepdims=True)
        f_col = jnp.sum(jnp.where(lane == h + N_HEADS, gcol, 0.0), axis=-1, keepdims=True)
        li_row = jnp.sum(jnp.where(sub == h, grow, 0.0), axis=0, keepdims=True)
        f_row = jnp.sum(jnp.where(sub == h + N_HEADS, grow, 0.0), axis=0, keepdims=True)
        lf_col = _log_sigmoid(f_col)
        lf_row = _log_sigmoid(f_row)
        b_col = jnp.sum(jnp.where(causal, lf_row, 0.0), axis=-1, keepdims=True)
        b_row = jnp.sum(jnp.where(upper, lf_col, 0.0), axis=0, keepdims=True)

        m_prev = m_ref[h, 0:1, 0:1]
        d = jnp.where(causal, b_col - b_row + li_row, NEG_BIG)
        inter = b_col + m_prev
        m_row = jnp.maximum(inter, jnp.max(d, axis=-1, keepdims=True))
        w_intra = jnp.exp(d - m_row)
        w_inter = jnp.exp(inter - m_row)

        s = lax.dot_general(qb, k.astype(BF16), (((1,), (1,)), ((), ())),
                            preferred_element_type=F32) * w_intra
        num = (jnp.dot(s.astype(BF16), v, preferred_element_type=F32)
               + w_inter * jnp.dot(qb, c_ref[h].astype(BF16), preferred_element_type=F32))
        den = (jnp.sum(s, axis=-1, keepdims=True)
               + w_inter * jnp.sum(q * n_ref[h], axis=-1, keepdims=True))
        hval = num / jnp.maximum(jnp.abs(den), jnp.exp(-m_row))

        b_last = b_col[L - 1:L, :]
        g_row = b_last - b_row + li_row
        g_col = b_last - b_col + li_col
        m_new = jnp.maximum(b_last + m_prev, jnp.max(g_row, axis=-1, keepdims=True))
        decay = jnp.exp(b_last + m_prev - m_new)
        kw = k * jnp.exp(g_col - m_new)
        c_ref[h] = decay * c_ref[h] + lax.dot_general(
            kw.astype(BF16), v, (((0,), (0,)), ((), ())), preferred_element_type=F32)
        n_ref[h] = decay * n_ref[h] + jnp.sum(kw, axis=0, keepdims=True)
        m_ref[h] = jnp.broadcast_to(m_new, m_ref.shape[1:])

        hm = o_ref[:, hs].astype(F32) * hval
        hm = hm * lax.rsqrt(jnp.mean(hm * hm, axis=-1, keepdims=True) + EPS) * gh_ref[:, hs]
        out_ref[:, hs] = (hm * z_ref[:, hs].astype(F32)).astype(out_ref.dtype)


def _mlstm_branch(ids, sg, zs, col, cw_b, perm, gif, gif_t, g_head, batch, seq):
    n = ids.shape[0]
    L = ML_CHUNK
    nc = seq // L
    hb = L // QK_HALO

    def tok(col_group):
        return pl.BlockSpec((L, D_MODEL), lambda b, c: (b * nc + c, col_group))

    def prev(col_group):
        return pl.BlockSpec((QK_HALO, D_MODEL),
                            lambda b, c: (jnp.maximum((b * nc + c) * hb - 1, 0), col_group))

    return pl.pallas_call(
        _mlstm_kernel,
        out_shape=jax.ShapeDtypeStruct((n, D_MODEL), BF16),
        grid=(batch, nc),
        in_specs=[tok(col["q"]), tok(col["k"]), prev(col["q"]), prev(col["k"]), tok(col["v"]),
                  pl.BlockSpec((QK_CONV_WIDTH, SUBLANES, 2 * D_MODEL), lambda b, c: (0, 0, 0)),
                  pl.BlockSpec((L, L), lambda b, c: (0, 0)),
                  pl.BlockSpec((L, LANES), lambda b, c: (b * nc + c, 0)),
                  pl.BlockSpec((1, SUBLANES, L), lambda b, c: (b, 0, c)),
                  tok(col["o"]), tok(col["z_ml"]),
                  pl.BlockSpec((1, D_MODEL), lambda b, c: (0, 0))],
        out_specs=pl.BlockSpec((L, D_MODEL), lambda b, c: (b * nc + c, 0)),
        scratch_shapes=[pltpu.VMEM((N_HEADS, HEAD_DIM, HEAD_DIM), F32),
                        pltpu.VMEM((N_HEADS, 1, HEAD_DIM), F32),
                        pltpu.VMEM((N_HEADS, SUBLANES, LANES), F32),
                        pltpu.VMEM((2, L + QK_HALO + SUBLANES, LANES), F32),
                        pltpu.VMEM((L, D_MODEL), BF16),
                        pltpu.VMEM((L, D_MODEL), BF16)],
        compiler_params=_params("parallel", "arbitrary"),
        name="mlstm",
    )(ids, ids, ids, ids, ids, cw_b, perm, gif, gif_t, sg, zs, g_head)


def _xattn_kernel(q_ref, k_ref, v_ref, z_ref, o_ref):
    scale = HEAD_DIM ** -0.5
    for h in range(N_HEADS):
        sl = slice(h * HEAD_DIM, (h + 1) * HEAD_DIM)
        s = lax.dot_general(q_ref[:, sl], k_ref[:, sl], (((1,), (1,)), ((), ())),
                            preferred_element_type=F32) * scale
        e = jnp.exp(s - jnp.max(s, axis=-1, keepdims=True))
        p = e / jnp.sum(e, axis=-1, keepdims=True)
        o = jnp.dot(p.astype(BF16), v_ref[:, sl], preferred_element_type=F32)
        o_ref[:, sl] = (o * z_ref[:, sl].astype(F32)).astype(o_ref.dtype)


def _xattn_branch(ids, q_col, zs, z_col, kv, mem_len, tm, seq):
    n = ids.shape[0]
    tiles_per_seq = seq // tm
    return pl.pallas_call(
        _xattn_kernel,
        out_shape=jax.ShapeDtypeStruct((n, D_MODEL), BF16),
        grid=(n // tm,),
        in_specs=[pl.BlockSpec((tm, D_MODEL), lambda i: (i, q_col)),
                  pl.BlockSpec((mem_len, D_MODEL), lambda i: (i // tiles_per_seq, 0)),
                  pl.BlockSpec((mem_len, D_MODEL), lambda i: (i // tiles_per_seq, 1)),
                  pl.BlockSpec((tm, D_MODEL), lambda i: (i, z_col))],
        out_specs=pl.BlockSpec((tm, D_MODEL), lambda i: (i, 0)),
        compiler_params=_params("parallel"),
        name="xattn",
    )(ids, kv, kv, zs)


def _merge_kernel(ac_ref, am_ref, ax_ref, wc_ref, wm_ref, wx_ref, gc_ref, gm_ref, gx_ref, o_ref):
    acc = gc_ref[...].astype(F32) * jnp.dot(ac_ref[...], wc_ref[...], preferred_element_type=F32)
    acc = acc + gm_ref[...].astype(F32) * jnp.dot(am_ref[...], wm_ref[...], preferred_element_type=F32)
    acc = acc + gx_ref[...].astype(F32) * jnp.dot(ax_ref[...], wx_ref[...], preferred_element_type=F32)
    o_ref[...] = acc.astype(o_ref.dtype)


def _merge(a_c, a_m, a_x, w_c, w_m, w_x, sg, gate_col, tm, tn):
    n, d = a_c.shape
    ncol = D_MODEL // tn
    act = pl.BlockSpec((tm, d), lambda j, i: (i, 0))
    wsp = pl.BlockSpec((d, tn), lambda j, i: (0, j))

    def gate(k):
        return pl.BlockSpec((tm, tn), lambda j, i: (i, (gate_col + k) * ncol + j))

    return pl.pallas_call(
        _merge_kernel,
        out_shape=jax.ShapeDtypeStruct((n, D_MODEL), BF16),
        grid=(ncol, n // tm),
        in_specs=[act, act, act, wsp, wsp, wsp, gate(0), gate(1), gate(2)],
        out_specs=pl.BlockSpec((tm, tn), lambda j, i: (i, j)),
        compiler_params=_params("parallel", "parallel"),
        name="merge",
    )(a_c, a_m, a_x, w_c, w_m, w_x, sg, sg, sg)


def _final_kernel(m_ref, w_ref, g_ref, x_ref, o_ref):
    y = jnp.dot(m_ref[...], w_ref[...], preferred_element_type=F32)
    y = y * lax.rsqrt(jnp.mean(y * y, axis=-1, keepdims=True) + EPS) * g_ref[...]
    o_ref[...] = x_ref[...] + y


def _final(merged, w_out, g_post, x2, tm):
    n, d = x2.shape
    return pl.pallas_call(
        _final_kernel,
        out_shape=jax.ShapeDtypeStruct((n, d), F32),
        grid=(n // tm,),
        in_specs=[pl.BlockSpec((tm, d), lambda i: (i, 0)),
                  pl.BlockSpec((d, d), lambda i: (0, 0)),
                  pl.BlockSpec((1, d), lambda i: (0, 0)),
                  pl.BlockSpec((tm, d), lambda i: (i, 0))],
        out_specs=pl.BlockSpec((tm, d), lambda i: (i, 0)),
        compiler_params=_params("parallel"),
        name="final",
    )(merged, w_out, g_post, x2)


def kernel(x, mem, g_pre, w_in, b_if, w_qk_conv, w_dw, b_dw, g_ln, b_ln, w_conv_out, g_ml_head,
           w_ml_out, g_mem, w_mem_kv, w_xa_out, w_out, g_post):
    batch, seq, d = x.shape
    mem_len = mem.shape[1]
    n = batch * seq
    assert d == D_MODEL and seq % ML_CHUNK == 0

    nif = 2 * N_HEADS
    w_in_t = w_in.T

    def grp(g):
        start = g * D_MODEL + (nif if g >= 8 else 0)
        return w_in_t[start:start + D_MODEL]

    order = (0, 1, 2, 7, 9, 3, 4, 5, 8, 6, 10, 11, 12)
    w_main = jnp.concatenate([grp(g) for g in order], axis=0).astype(BF16)
    col = dict(z_conv=0, z_ml=1, z_xa=2, q=0, k=1, v=2, q_xa=3, o=0, gates=1)

    w_if = jnp.pad(w_in[:, 8 * D_MODEL:8 * D_MODEL + nif], ((0, 0), (0, LANES - nif)))
    w_if_hi = w_if.astype(BF16)
    w_if_lo = (w_if - w_if_hi.astype(F32)).astype(BF16)
    w_if_hl = jnp.concatenate([w_if_hi, w_if_lo], axis=1)
    b_if_pad = jnp.pad(b_if, (0, LANES - nif)).reshape(1, LANES)

    def row(vec):
        return vec.reshape(1, -1).astype(F32)

    def lane_cols(vec):
        return vec.reshape(d // LANES, 1, LANES).astype(F32)

    x2 = x.reshape(n, d)
    h, gif = _prenorm(x2, row(g_pre), w_if_hl, b_if_pad, tm=512)
    gif_t = gif[:, :SUBLANES].reshape(batch, seq, SUBLANES).transpose(0, 2, 1)

    u = _glu(h, w_main, tm=1024, tn=1024)
    zs = _proj(h, w_main, 2 * D_MODEL, 3 * D_MODEL, _silu, tm=1024, tn=2048)
    ids = _proj(h, w_main, 5 * D_MODEL, 4 * D_MODEL, _identity, tm=1024, tn=2048)
    sg = _proj(h, w_main, 9 * D_MODEL, 4 * D_MODEL, _sigmoid, tm=1024, tn=2048)

    w_dw_b = jnp.broadcast_to(w_dw.astype(F32)[:, None, :], (CONV_WIDTH, SUBLANES, d))
    a_c = _conv_branch(u, zs, col["z_conv"], w_dw_b, lane_cols(b_dw), lane_cols(g_ln),
                       lane_cols(b_ln), tm=512, seq=seq)

    cw_b = jnp.broadcast_to(w_qk_conv.astype(F32)[:, None, :], (QK_CONV_WIDTH, SUBLANES, 2 * d))
    p_idx = jnp.arange(ML_CHUNK)
    t_of_p = (p_idx % SUBLANES) * (ML_CHUNK // SUBLANES) + p_idx // SUBLANES
    perm = (jnp.arange(ML_CHUNK)[:, None] == t_of_p[None, :]).astype(BF16)
    a_m = _mlstm_branch(ids, sg, zs, col, cw_b, perm, gif, gif_t, row(g_ml_head), batch, seq)

    mem_h = _memnorm(mem.reshape(batch * mem_len, d), row(g_mem), tm=256)
    kv = _proj(mem_h, w_mem_kv.T.astype(BF16), 0, 2 * D_MODEL, _identity, tm=256, tn=2048)
    a_x = _xattn_branch(ids, col["q_xa"], zs, col["z_xa"], kv, mem_len, tm=512, seq=seq)

    merged = _merge(a_c, a_m, a_x, w_conv_out.astype(BF16), w_ml_out.astype(BF16),
                    w_xa_out.astype(BF16), sg, col["gates"], tm=512, tn=1024)
    out = _final(merged, w_out.astype(BF16), row(g_post), x2, tm=512)
    return out.reshape(batch, seq, d)
```

```python
import functools

import jax
import jax.numpy as jnp
from jax import lax
from jax.experimental import pallas as pl
from jax.experimental.pallas import tpu as pltpu

F32 = jnp.float32
BF16 = jnp.bfloat16

D_MODEL = 2048
N_HEADS = 4
HEAD_DIM = D_MODEL // N_HEADS
CONV_WIDTH = 31
QK_CONV_WIDTH = 4
EPS = 1e-6
NEG_BIG = -1e30

LANES = 128
SUBLANES = 8
VMEM_LIMIT = 56 * 1024 * 1024


def _sigmoid(y):
    return 1.0 / (1.0 + jnp.exp(-y))


def _silu(y):
    return y * _sigmoid(y)


def _sigmoid_t(y):
    return 0.5 * jnp.tanh(0.5 * y) + 0.5


def _silu_t(y):
    t = 0.5 * y
    return t * (jnp.tanh(t) + 1.0)


def _identity(y):
    return y


def _params(*sem):
    return pltpu.CompilerParams(dimension_semantics=sem, vmem_limit_bytes=VMEM_LIMIT)


def _prenorm_kernel(x_ref, g_ref, whl_ref, bif_ref, h_ref, gif_ref, gift_ref):
    xf = x_ref[...]
    y = xf * lax.rsqrt(jnp.mean(xf * xf, axis=-1, keepdims=True) + EPS) * g_ref[...]
    hi = y.astype(BF16)
    h_ref[...] = hi
    lo = (y - hi.astype(F32)).astype(BF16)
    hh = jnp.dot(hi, whl_ref[...], preferred_element_type=F32)
    lh = jnp.dot(lo, whl_ref[:, 0:LANES], preferred_element_type=F32)
    gif = hh[:, 0:LANES] + (hh[:, LANES:] + lh) + bif_ref[...]
    gif_ref[...] = gif
    gift_ref[0] = gif.T[0:SUBLANES, :]


def _prenorm(x2, g, w_if_hl, bif, tm, batch):
    n, d = x2.shape
    tiles_per_seq = n // batch // tm
    return pl.pallas_call(
        _prenorm_kernel,
        out_shape=(jax.ShapeDtypeStruct((n, d), BF16), jax.ShapeDtypeStruct((n, LANES), F32),
                   jax.ShapeDtypeStruct((batch, SUBLANES, n // batch), F32)),
        grid=(n // tm,),
        in_specs=[pl.BlockSpec((tm, d), lambda i: (i, 0)),
                  pl.BlockSpec((1, d), lambda i: (0, 0)),
                  pl.BlockSpec((d, 2 * LANES), lambda i: (0, 0)),
                  pl.BlockSpec((1, LANES), lambda i: (0, 0))],
        out_specs=(pl.BlockSpec((tm, d), lambda i: (i, 0)),
                   pl.BlockSpec((tm, LANES), lambda i: (i, 0)),
                   pl.BlockSpec((1, SUBLANES, tm),
                                lambda i: (i // tiles_per_seq, 0, i % tiles_per_seq))),
        compiler_params=_params("parallel"),
        name="prenorm",
    )(x2, g, w_if_hl, bif)


def _memnorm_kernel(x_ref, g_ref, h_ref):
    xf = x_ref[...]
    y = xf * lax.rsqrt(jnp.mean(xf * xf, axis=-1, keepdims=True) + EPS) * g_ref[...]
    h_ref[...] = y.astype(BF16)


def _memnorm(x2, g, tm):
    n, d = x2.shape
    return pl.pallas_call(
        _memnorm_kernel,
        out_shape=jax.ShapeDtypeStruct((n, d), BF16),
        grid=(n // tm,),
        in_specs=[pl.BlockSpec((tm, d), lambda i: (i, 0)),
                  pl.BlockSpec((1, d), lambda i: (0, 0))],
        out_specs=pl.BlockSpec((tm, d), lambda i: (i, 0)),
        compiler_params=_params("parallel"),
        name="memnorm",
    )(x2, g)


PROJ_SUB = 512


def _dot_t(a, w_t):
    return lax.dot_general(a, w_t, (((1,), (1,)), ((), ())), preferred_element_type=F32)


def _proj_kernel(h_ref, wt_ref, o_ref, *, act):
    hh = h_ref[...]
    for c0 in range(0, o_ref.shape[1], PROJ_SUB):
        sl = slice(c0, c0 + PROJ_SUB)
        o_ref[:, sl] = act(_dot_t(hh, wt_ref[sl, :])).astype(o_ref.dtype)


def _proj(h, w_t, row0, ncols, act, tm, tn):
    n, d = h.shape
    off = row0 // tn
    return pl.pallas_call(
        functools.partial(_proj_kernel, act=act),
        out_shape=jax.ShapeDtypeStruct((n, ncols), BF16),
        grid=(ncols // tn, n // tm),
        in_specs=[pl.BlockSpec((tm, d), lambda j, i: (i, 0)),
                  pl.BlockSpec((tn, d), lambda j, i: (j + off, 0))],
        out_specs=pl.BlockSpec((tm, tn), lambda j, i: (i, j)),
        compiler_params=_params("parallel", "parallel"),
        name="proj",
    )(h, w_t)


def _glu_kernel(h_ref, wa_ref, wb_ref, o_ref):
    hh = h_ref[...]
    a = _dot_t(hh, wa_ref[...])
    b = _dot_t(hh, wb_ref[...])
    o_ref[...] = (a * _sigmoid_t(b)).astype(o_ref.dtype)


def _glu(h, w_t, tm, tn):
    n, d = h.shape
    ncol = D_MODEL // tn
    return pl.pallas_call(
        _glu_kernel,
        out_shape=jax.ShapeDtypeStruct((n, D_MODEL), BF16),
        grid=(ncol, n // tm),
        in_specs=[pl.BlockSpec((tm, d), lambda j, i: (i, 0)),
                  pl.BlockSpec((tn, d), lambda j, i: (j, 0)),
                  pl.BlockSpec((tn, d), lambda j, i: (j + ncol, 0))],
        out_specs=pl.BlockSpec((tm, tn), lambda j, i: (i, j)),
        compiler_params=_params("parallel", "parallel"),
        name="proj_glu",
    )(h, w_t, w_t)


CONV_HALO = 32
CONV_KB = 8
CONV_RB = 16


def _conv_kernel(u_ref, prev_ref, z_ref, w_ref, bdw_ref, gln_ref, bln_ref, o_ref,
                 slab_ref, acc_ref, *, tm, tiles_per_seq):
    i = pl.program_id(0)
    first = (i % tiles_per_seq) == 0
    seg = tm // SUBLANES
    pitch = seg + 1
    base = CONV_HALO - (CONV_WIDTH - 1)
    assert CONV_HALO < seg and seg % CONV_HALO == 0

    def gap(nrow):
        return nrow + nrow // seg

    def lane_col(c, carry):
        l0 = pl.multiple_of(c * LANES, LANES)
        lanes = pl.ds(l0, LANES)
        slab_ref[0:CONV_HALO, :] = jnp.where(first, 0.0, prev_ref[:, lanes].astype(F32))
        slab_ref[CONV_HALO:seg, :] = u_ref[0:seg - CONV_HALO, lanes].astype(F32)
        for q in range(1, SUBLANES):
            slab_ref[q * pitch:q * pitch + seg, :] = (
                u_ref[q * seg - CONV_HALO:(q + 1) * seg - CONV_HALO, lanes].astype(F32))
        slab_ref[SUBLANES * pitch:SUBLANES * pitch + CONV_HALO, :] = (
            u_ref[tm - CONV_HALO:tm, lanes].astype(F32))
        for k0 in range(0, seg, CONV_KB):
            data = [slab_ref[pl.ds(gap(k0 + base + m), SUBLANES, stride=pitch), :]
                    for m in range(CONV_KB + CONV_WIDTH - 1)]
            accs = [jnp.zeros((SUBLANES, LANES), F32) for _ in range(CONV_KB)]
            for j in range(CONV_WIDTH):
                wj = w_ref[j, :, lanes]
                for kk in range(CONV_KB):
                    accs[kk] = accs[kk] + wj * data[kk + j]
            for kk in range(CONV_KB):
                acc_ref[c, pl.ds(k0 + kk, SUBLANES, stride=seg), :] = accs[kk]
        return carry

    lax.fori_loop(0, D_MODEL // LANES, lane_col, 0)

    def row_block(rb, carry):
        r0 = pl.multiple_of(rb * CONV_RB, CONV_RB)
        rows = pl.ds(r0, CONV_RB)
        y = acc_ref[:, rows, :] + bdw_ref[...]
        mu = jnp.sum(jnp.sum(y, axis=0), axis=-1, keepdims=True) * (1.0 / D_MODEL)
        yc = y - mu
        var = jnp.sum(jnp.sum(yc * yc, axis=0), axis=-1, keepdims=True) * (1.0 / D_MODEL)
        yn = yc * lax.rsqrt(var + EPS) * gln_ref[...] + bln_ref[...]
        for c in range(D_MODEL // LANES):
            sl = slice(c * LANES, (c + 1) * LANES)
            o_ref[rows, sl] = (_silu(yn[c]) * z_ref[rows, sl].astype(F32)).astype(o_ref.dtype)
        return carry

    lax.fori_loop(0, tm // CONV_RB, row_block, 0, unroll=8)


def _conv_branch(u, z, z_col, w_dw_b, b_dw, g_ln, b_ln, tm, seq):
    n, d = u.shape
    tiles_per_seq = seq // tm
    hb = tm // CONV_HALO
    vec = pl.BlockSpec((d // LANES, 1, LANES), lambda i: (0, 0, 0))
    return pl.pallas_call(
        functools.partial(_conv_kernel, tm=tm, tiles_per_seq=tiles_per_seq),
        out_shape=jax.ShapeDtypeStruct((n, d), BF16),
        grid=(n // tm,),
        in_specs=[pl.BlockSpec((tm, d), lambda i: (i, 0)),
                  pl.BlockSpec((CONV_HALO, d), lambda i: (jnp.maximum(i * hb - 1, 0), 0)),
                  pl.BlockSpec((tm, d), lambda i: (i, z_col)),
                  pl.BlockSpec((CONV_WIDTH, SUBLANES, d), lambda i: (0, 0, 0)),
                  vec, vec, vec],
        out_specs=pl.BlockSpec((tm, d), lambda i: (i, 0)),
        scratch_shapes=[pltpu.VMEM((tm + CONV_HALO + SUBLANES, LANES), F32),
                        pltpu.VMEM((d // LANES, tm, LANES), F32)],
        compiler_params=_params("parallel"),
        name="conv_branch",
    )(u, u, z, w_dw_b, b_dw, g_ln, b_ln)


ML_CHUNK = 256
QK_HALO = 16


def _log_sigmoid(t):
    return jnp.minimum(t, 0.0) - jnp.log(1.0 + jnp.exp(-jnp.abs(t)))


def _mlstm_kernel(q_ref, k_ref, qprev_ref, kprev_ref, v_ref, cw_ref, perm_ref, gc_ref, gr_ref,
                  o_ref, z_ref, gh_ref, out_ref, c_ref, n_ref, m_ref, slab_ref, qp_ref, kp_ref):
    first = pl.program_id(1) == 0
    L = ML_CHUNK
    seg = L // SUBLANES
    pitch = seg + 1
    base = QK_HALO - (QK_CONV_WIDTH - 1)
    assert QK_HALO < seg and seg % QK_HALO == 0

    @pl.when(first)
    def _():
        c_ref[...] = jnp.zeros_like(c_ref)
        n_ref[...] = jnp.zeros_like(n_ref)
        m_ref[...] = jnp.zeros_like(m_ref)

    def gap(nrow):
        return nrow + nrow // seg

    def conv_col(lc, carry):
        l0 = pl.multiple_of(lc * LANES, LANES)
        lanes = pl.ds(l0, LANES)
        for a, (pre_ref, prev_ref, dst_ref, scale) in enumerate((
                (q_ref, qprev_ref, qp_ref, 1.0), (k_ref, kprev_ref, kp_ref, HEAD_DIM ** -0.5))):
            slab_ref[a, 0:QK_HALO, :] = jnp.where(first, 0.0, prev_ref[:, lanes].astype(F32))
            slab_ref[a, QK_HALO:seg, :] = pre_ref[0:seg - QK_HALO, lanes].astype(F32)
            for s in range(1, SUBLANES):
                slab_ref[a, s * pitch:s * pitch + seg, :] = (
                    pre_ref[s * seg - QK_HALO:(s + 1) * seg - QK_HALO, lanes].astype(F32))
            slab_ref[a, SUBLANES * pitch:SUBLANES * pitch + QK_HALO, :] = (
                pre_ref[L - QK_HALO:L, lanes].astype(F32))
            data = [slab_ref[a, pl.ds(gap(base + m), SUBLANES, stride=pitch), :]
                    for m in range(seg + QK_CONV_WIDTH - 1)]
            wl = pl.ds(pl.multiple_of(a * D_MODEL + l0, LANES), LANES)
            w = [cw_ref[j, :, wl] for j in range(QK_CONV_WIDTH)]
            outs = []
            for kk in range(seg):
                acc = w[0] * data[kk]
                for j in range(1, QK_CONV_WIDTH):
                    acc = acc + w[j] * data[kk + j]
                outs.append(_silu(acc) * scale)
            for kk in range(0, seg, 2):
                dst_ref[kk * SUBLANES:(kk + 2) * SUBLANES, lanes] = (
                    jnp.concatenate(outs[kk:kk + 2], axis=0).astype(BF16))
        return carry

    lax.fori_loop(0, D_MODEL // LANES, conv_col, 0)

    perm = perm_ref[...]
    ri = lax.broadcasted_iota(jnp.int32, (L, L), 0)
    ci = lax.broadcasted_iota(jnp.int32, (L, L), 1)
    causal = ri >= ci
    upper = ri <= ci
    gcol = gc_ref[...]
    lane = lax.broadcasted_iota(jnp.int32, gcol.shape, 1)
    grow = gr_ref[0]
    sub = lax.broadcasted_iota(jnp.int32, grow.shape, 0)

    for h in range(N_HEADS):
        hs = slice(h * HEAD_DIM, (h + 1) * HEAD_DIM)
        q = jnp.dot(perm, qp_ref[:, hs], preferred_element_type=F32)
        k = jnp.dot(perm, kp_ref[:, hs], preferred_element_type=F32)
        v = v_ref[:, hs]
        qb = q.astype(BF16)

        li_col = jnp.sum(jnp.where(lane == h, gcol, 0.0), axis=-1, keepdims=True)
        f_col = jnp.sum(jnp.where(lane == h + N_HEADS, gcol, 0.0), axis=-1, keepdims=True)
        li_row = jnp.sum(jnp.where(sub == h, grow, 0.0), axis=0, keepdims=True)
        f_row = jnp.sum(jnp.where(sub == h + N_HEADS, grow, 0.0), axis=0, keepdims=True)
        lf_col = _log_sigmoid(f_col)
        lf_row = _log_sigmoid(f_row)
        b_col = jnp.sum(jnp.where(causal, lf_row, 0.0), axis=-1, keepdims=True)
        b_row = jnp.sum(jnp.where(upper, lf_col, 0.0), axis=0, keepdims=True)

        m_prev = m_ref[h, 0:1, 0:1]
        d = jnp.where(causal, b_col - b_row + li_row, NEG_BIG)
        inter = b_col + m_prev
        m_row = jnp.maximum(inter, jnp.max(d, axis=-1, keepdims=True))
        w_intra = jnp.exp(d - m_row)
        w_inter = jnp.exp(inter - m_row)

        s = lax.dot_general(qb, k.astype(BF16), (((1,), (1,)), ((), ())),
                            preferred_element_type=F32) * w_intra
        num = (jnp.dot(s.astype(BF16), v, preferred_element_type=F32)
               + w_inter * jnp.dot(qb, c_ref[h].astype(BF16), preferred_element_type=F32))
        den = (jnp.sum(s, axis=-1, keepdims=True)
               + w_inter * jnp.sum(q * n_ref[h], axis=-1, keepdims=True))
        hval = num / jnp.maximum(jnp.abs(den), jnp.exp(-m_row))

        b_last = b_col[L - 1:L, :]
        g_row = b_last - b_row + li_row
        g_col = b_last - b_col + li_col
        m_new = jnp.maximum(b_last + m_prev, jnp.max(g_row, axis=-1, keepdims=True))
        decay = jnp.exp(b_last + m_prev - m_new)
        kw = k * jnp.exp(g_col - m_new)
        c_ref[h] = decay * c_ref[h] + lax.dot_general(
            kw.astype(BF16), v, (((0,), (0,)), ((), ())), preferred_element_type=F32)
        n_ref[h] = decay * n_ref[h] + jnp.sum(kw, axis=0, keepdims=True)
        m_ref[h] = jnp.broadcast_to(m_new, m_ref.shape[1:])

        hm = o_ref[:, hs].astype(F32) * hval
        hm = hm * lax.rsqrt(jnp.mean(hm * hm, axis=-1, keepdims=True) + EPS) * gh_ref[:, hs]
        out_ref[:, hs] = (hm * z_ref[:, hs].astype(F32)).astype(out_ref.dtype)


def _mlstm_branch(ids, sg, zs, col, cw_b, perm, gif, gif_t, g_head, batch, seq):
    n = ids.shape[0]
    L = ML_CHUNK
    nc = seq // L
    hb = L // QK_HALO

    def tok(col_group):
        return pl.BlockSpec((L, D_MODEL), lambda b, c: (b * nc + c, col_group))

    def prev(col_group):
        return pl.BlockSpec((QK_HALO, D_MODEL),
                            lambda b, c: (jnp.maximum((b * nc + c) * hb - 1, 0), col_group))

    return pl.pallas_call(
        _mlstm_kernel,
        out_shape=jax.ShapeDtypeStruct((n, D_MODEL), BF16),
        grid=(batch, nc),
        in_specs=[tok(col["q"]), tok(col["k"]), prev(col["q"]), prev(col["k"]), tok(col["v"]),
                  pl.BlockSpec((QK_CONV_WIDTH, SUBLANES, 2 * D_MODEL), lambda b, c: (0, 0, 0)),
                  pl.BlockSpec((L, L), lambda b, c: (0, 0)),
                  pl.BlockSpec((L, LANES), lambda b, c: (b * nc + c, 0)),
                  pl.BlockSpec((1, SUBLANES, L), lambda b, c: (b, 0, c)),
                  tok(col["o"]), tok(col["z_ml"]),
                  pl.BlockSpec((1, D_MODEL), lambda b, c: (0, 0))],
        out_specs=pl.BlockSpec((L, D_MODEL), lambda b, c: (b * nc + c, 0)),
        scratch_shapes=[pltpu.VMEM((N_HEADS, HEAD_DIM, HEAD_DIM), F32),
                        pltpu.VMEM((N_HEADS, 1, HEAD_DIM), F32),
                        pltpu.VMEM((N_HEADS, SUBLANES, LANES), F32),
                        pltpu.VMEM((2, L + QK_HALO + SUBLANES, LANES), F32),
                        pltpu.VMEM((L, D_MODEL), BF16),
                        pltpu.VMEM((L, D_MODEL), BF16)],
        compiler_params=_params("parallel", "arbitrary"),
        name="mlstm",
    )(ids, ids, ids, ids, ids, cw_b, perm, gif, gif_t, sg, zs, g_head)


def _xattn_kernel(q_ref, k_ref, v_ref, z_ref, o_ref):
    scale = HEAD_DIM ** -0.5
    for h in range(N_HEADS):
        sl = slice(h * HEAD_DIM, (h + 1) * HEAD_DIM)
        s = lax.dot_general(q_ref[:, sl], k_ref[:, sl], (((1,), (1,)), ((), ())),
                            preferred_element_type=F32) * scale
        e = jnp.exp(s - jnp.max(s, axis=-1, keepdims=True))
        p = e / jnp.sum(e, axis=-1, keepdims=True)
        o = jnp.dot(p.astype(BF16), v_ref[:, sl], preferred_element_type=F32)
        o_ref[:, sl] = (o * z_ref[:, sl].astype(F32)).astype(o_ref.dtype)


def _xattn_branch(ids, q_col, zs, z_col, kv, mem_len, tm, seq):
    n = ids.shape[0]
    tiles_per_seq = seq // tm
    return pl.pallas_call(
        _xattn_kernel,
        out_shape=jax.ShapeDtypeStruct((n, D_MODEL), BF16),
        grid=(n // tm,),
        in_specs=[pl.BlockSpec((tm, D_MODEL), lambda i: (i, q_col)),
                  pl.BlockSpec((mem_len, D_MODEL), lambda i: (i // tiles_per_seq, 0)),
                  pl.BlockSpec((mem_len, D_MODEL), lambda i: (i // tiles_per_seq, 1)),
                  pl.BlockSpec((tm, D_MODEL), lambda i: (i, z_col))],
        out_specs=pl.BlockSpec((tm, D_MODEL), lambda i: (i, 0)),
        compiler_params=_params("parallel"),
        name="xattn",
    )(ids, kv, kv, zs)


def _merge_kernel(ac_ref, am_ref, ax_ref, wc_ref, wm_ref, wx_ref, gc_ref, gm_ref, gx_ref, o_ref):
    acc = gc_ref[...].astype(F32) * jnp.dot(ac_ref[...], wc_ref[...], preferred_element_type=F32)
    acc = acc + gm_ref[...].astype(F32) * jnp.dot(am_ref[...], wm_ref[...], preferred_element_type=F32)
    acc = acc + gx_ref[...].astype(F32) * jnp.dot(ax_ref[...], wx_ref[...], preferred_element_type=F32)
    o_ref[...] = acc.astype(o_ref.dtype)


def _merge(a_c, a_m, a_x, w_c, w_m, w_x, sg, gate_col, tm, tn):
    n, d = a_c.shape
    ncol = D_MODEL // tn
    act = pl.BlockSpec((tm, d), lambda j, i: (i, 0))
    wsp = pl.BlockSpec((d, tn), lambda j, i: (0, j))

    def gate(k):
        return pl.BlockSpec((tm, tn), lambda j, i: (i, (gate_col + k) * ncol + j))

    return pl.pallas_call(
        _merge_kernel,
        out_shape=jax.ShapeDtypeStruct((n, D_MODEL), BF16),
        grid=(ncol, n // tm),
        in_specs=[act, act, act, wsp, wsp, wsp, gate(0), gate(1), gate(2)],
        out_specs=pl.BlockSpec((tm, tn), lambda j, i: (i, j)),
        compiler_params=_params("parallel", "parallel"),
        name="merge",
    )(a_c, a_m, a_x, w_c, w_m, w_x, sg, sg, sg)


def _final_kernel(m_ref, w_ref, g_ref, x_ref, o_ref):
    y = jnp.dot(m_ref[...], w_ref[...], preferred_element_type=F32)
    y = y * lax.rsqrt(jnp.mean(y * y, axis=-1, keepdims=True) + EPS) * g_ref[...]
    o_ref[...] = x_ref[...] + y


def _final(merged, w_out, g_post, x2, tm):
    n, d = x2.shape
    return pl.pallas_call(
        _final_kernel,
        out_shape=jax.ShapeDtypeStruct((n, d), F32),
        grid=(n // tm,),
        in_specs=[pl.BlockSpec((tm, d), lambda i: (i, 0)),
                  pl.BlockSpec((d, d), lambda i: (0, 0)),
                  pl.BlockSpec((1, d), lambda i: (0, 0)),
                  pl.BlockSpec((tm, d), lambda i: (i, 0))],
        out_specs=pl.BlockSpec((tm, d), lambda i: (i, 0)),
        compiler_params=_params("parallel"),
        name="final",
    )(merged, w_out, g_post, x2)


def kernel(x, mem, g_pre, w_in, b_if, w_qk_conv, w_dw, b_dw, g_ln, b_ln, w_conv_out, g_ml_head,
           w_ml_out, g_mem, w_mem_kv, w_xa_out, w_out, g_post):
    batch, seq, d = x.shape
    mem_len = mem.shape[1]
    n = batch * seq
    assert d == D_MODEL and seq % ML_CHUNK == 0

    nif = 2 * N_HEADS
    w_in_t = w_in.T

    def grp(g):
        start = g * D_MODEL + (nif if g >= 8 else 0)
        return w_in_t[start:start + D_MODEL]

    order = (0, 1, 2, 7, 9, 3, 4, 5, 8, 6, 10, 11, 12)
    w_main = jnp.concatenate([grp(g) for g in order], axis=0).astype(BF16)
    col = dict(z_conv=0, z_ml=1, z_xa=2, q=0, k=1, v=2, q_xa=3, o=0, gates=1)

    w_if = jnp.pad(w_in[:, 8 * D_MODEL:8 * D_MODEL + nif], ((0, 0), (0, LANES - nif)))
    w_if_hi = w_if.astype(BF16)
    w_if_lo = (w_if - w_if_hi.astype(F32)).astype(BF16)
    w_if_hl = jnp.concatenate([w_if_hi, w_if_lo], axis=1)
    b_if_pad = jnp.pad(b_if, (0, LANES - nif)).reshape(1, LANES)

    def row(vec):
        return vec.reshape(1, -1).astype(F32)

    def lane_cols(vec):
        return vec.reshape(d // LANES, 1, LANES).astype(F32)

    x2 = x.reshape(n, d)
    h, gif, gif_t = _prenorm(x2, row(g_pre), w_if_hl, b_if_pad, tm=512, batch=batch)

    u = _glu(h, w_main, tm=1024, tn=1024)
    zs = _proj(h, w_main, 2 * D_MODEL, 3 * D_MODEL, _silu_t, tm=1024, tn=2048)
    ids = _proj(h, w_main, 5 * D_MODEL, 4 * D_MODEL, _identity, tm=1024, tn=2048)
    sg = _proj(h, w_main, 9 * D_MODEL, 4 * D_MODEL, _sigmoid_t, tm=1024, tn=2048)

    w_dw_b = jnp.broadcast_to(w_dw.astype(F32)[:, None, :], (CONV_WIDTH, SUBLANES, d))
    a_c = _conv_branch(u, zs, col["z_conv"], w_dw_b, lane_cols(b_dw), lane_cols(g_ln),
                       lane_cols(b_ln), tm=512, seq=seq)

    cw_b = jnp.broadcast_to(w_qk_conv.astype(F32)[:, None, :], (QK_CONV_WIDTH, SUBLANES, 2 * d))
    p_idx = jnp.arange(ML_CHUNK)
    t_of_p = (p_idx % SUBLANES) * (ML_CHUNK // SUBLANES) + p_idx // SUBLANES
    perm = (jnp.arange(ML_CHUNK)[:, None] == t_of_p[None, :]).astype(BF16)
    a_m = _mlstm_branch(ids, sg, zs, col, cw_b, perm, gif, gif_t, row(g_ml_head), batch, seq)

    mem_h = _memnorm(mem.reshape(batch * mem_len, d), row(g_mem), tm=256)
    kv = _proj(mem_h, w_mem_kv.T.astype(BF16), 0, 2 * D_MODEL, _identity, tm=256, tn=2048)
    a_x = _xattn_branch(ids, col["q_xa"], zs, col["z_xa"], kv, mem_len, tm=512, seq=seq)

    merged = _merge(a_c, a_m, a_x, w_conv_out.astype(BF16), w_ml_out.astype(BF16),
                    w_xa_out.astype(BF16), sg, col["gates"], tm=512, tn=1024)
    out = _final(merged, w_out.astype(BF16), row(g_post), x2, tm=512)
    return out.reshape(batch, seq, d)
```

```python
import functools

import jax
import jax.numpy as jnp
from jax import lax
from jax.experimental import pallas as pl
from jax.experimental.pallas import tpu as pltpu

F32 = jnp.float32
BF16 = jnp.bfloat16

D_MODEL = 2048
N_HEADS = 4
HEAD_DIM = D_MODEL // N_HEADS
CONV_WIDTH = 31
QK_CONV_WIDTH = 4
EPS = 1e-6
NEG_BIG = -1e30

LANES = 128
SUBLANES = 8
VMEM_LIMIT = 56 * 1024 * 1024


def _sigmoid(y):
    return 1.0 / (1.0 + jnp.exp(-y))


def _silu(y):
    return y * _sigmoid(y)


def _sigmoid_t(y):
    return 0.5 * jnp.tanh(0.5 * y) + 0.5


def _silu_t(y):
    t = 0.5 * y
    return t * (jnp.tanh(t) + 1.0)


def _identity(y):
    return y


def _params(*sem):
    return pltpu.CompilerParams(dimension_semantics=sem, vmem_limit_bytes=VMEM_LIMIT)


def _prenorm_kernel(x_ref, g_ref, whl_ref, bif_ref, h_ref, gif_ref, gift_ref):
    xf = x_ref[...]
    y = xf * lax.rsqrt(jnp.mean(xf * xf, axis=-1, keepdims=True) + EPS) * g_ref[...]
    hi = y.astype(BF16)
    h_ref[...] = hi
    lo = (y - hi.astype(F32)).astype(BF16)
    hh = jnp.dot(hi, whl_ref[...], preferred_element_type=F32)
    lh = jnp.dot(lo, whl_ref[:, 0:LANES], preferred_element_type=F32)
    gif = hh[:, 0:LANES] + (hh[:, LANES:] + lh) + bif_ref[...]
    gif_ref[...] = gif
    gift_ref[0] = gif.T[0:SUBLANES, :]


def _prenorm(x2, g, w_if_hl, bif, tm, batch):
    n, d = x2.shape
    tiles_per_seq = n // batch // tm
    return pl.pallas_call(
        _prenorm_kernel,
        out_shape=(jax.ShapeDtypeStruct((n, d), BF16), jax.ShapeDtypeStruct((n, LANES), F32),
                   jax.ShapeDtypeStruct((batch, SUBLANES, n // batch), F32)),
        grid=(n // tm,),
        in_specs=[pl.BlockSpec((tm, d), lambda i: (i, 0)),
                  pl.BlockSpec((1, d), lambda i: (0, 0)),
                  pl.BlockSpec((d, 2 * LANES), lambda i: (0, 0)),
                  pl.BlockSpec((1, LANES), lambda i: (0, 0))],
        out_specs=(pl.BlockSpec((tm, d), lambda i: (i, 0)),
                   pl.BlockSpec((tm, LANES), lambda i: (i, 0)),
                   pl.BlockSpec((1, SUBLANES, tm),
                                lambda i: (i // tiles_per_seq, 0, i % tiles_per_seq))),
        compiler_params=_params("parallel"),
        name="prenorm",
    )(x2, g, w_if_hl, bif)


def _memnorm_kernel(x_ref, g_ref, h_ref):
    xf = x_ref[...]
    y = xf * lax.rsqrt(jnp.mean(xf * xf, axis=-1, keepdims=True) + EPS) * g_ref[...]
    h_ref[...] = y.astype(BF16)


def _memnorm(x2, g, tm):
    n, d = x2.shape
    return pl.pallas_call(
        _memnorm_kernel,
        out_shape=jax.ShapeDtypeStruct((n, d), BF16),
        grid=(n // tm,),
        in_specs=[pl.BlockSpec((tm, d), lambda i: (i, 0)),
                  pl.BlockSpec((1, d), lambda i: (0, 0))],
        out_specs=pl.BlockSpec((tm, d), lambda i: (i, 0)),
        compiler_params=_params("parallel"),
        name="memnorm",
    )(x2, g)


PROJ_SUB = 512


def _dot_t(a, w_t):
    return lax.dot_general(a, w_t, (((1,), (1,)), ((), ())), preferred_element_type=F32)


def _proj_kernel(h_ref, w_ref, o_ref, *, act, out_major):
    hh = h_ref[...]
    for c0 in range(0, o_ref.shape[1], PROJ_SUB):
        sl = slice(c0, c0 + PROJ_SUB)
        y = (_dot_t(hh, w_ref[sl, :]) if out_major
             else jnp.dot(hh, w_ref[:, sl], preferred_element_type=F32))
        o_ref[:, sl] = act(y).astype(o_ref.dtype)


def _proj(h, w, start, ncols, act, tm, tn, out_major=True):
    n, d = h.shape
    off = start // tn
    wspec = (pl.BlockSpec((tn, d), lambda j, i: (j + off, 0)) if out_major
             else pl.BlockSpec((d, tn), lambda j, i: (0, j + off)))
    return pl.pallas_call(
        functools.partial(_proj_kernel, act=act, out_major=out_major),
        out_shape=jax.ShapeDtypeStruct((n, ncols), BF16),
        grid=(ncols // tn, n // tm),
        in_specs=[pl.BlockSpec((tm, d), lambda j, i: (i, 0)), wspec],
        out_specs=pl.BlockSpec((tm, tn), lambda j, i: (i, j)),
        compiler_params=_params("parallel", "parallel"),
        name="proj",
    )(h, w)


def _glu_kernel(h_ref, wa_ref, wb_ref, o_ref):
    hh = h_ref[...]
    a = _dot_t(hh, wa_ref[...])
    b = _dot_t(hh, wb_ref[...])
    o_ref[...] = (a * _sigmoid_t(b)).astype(o_ref.dtype)


def _glu(h, w_t, tm, tn):
    n, d = h.shape
    ncol = D_MODEL // tn
    return pl.pallas_call(
        _glu_kernel,
        out_shape=jax.ShapeDtypeStruct((n, D_MODEL), BF16),
        grid=(ncol, n // tm),
        in_specs=[pl.BlockSpec((tm, d), lambda j, i: (i, 0)),
                  pl.BlockSpec((tn, d), lambda j, i: (j, 0)),
                  pl.BlockSpec((tn, d), lambda j, i: (j + ncol, 0))],
        out_specs=pl.BlockSpec((tm, tn), lambda j, i: (i, j)),
        compiler_params=_params("parallel", "parallel"),
        name="proj_glu",
    )(h, w_t, w_t)


CONV_HALO = 32
CONV_KB = 8
CONV_RB = 16


def _conv_kernel(u_ref, prev_ref, z_ref, w_ref, bdw_ref, gln_ref, bln_ref, o_ref,
                 slab_ref, acc_ref, *, tm, tiles_per_seq):
    i = pl.program_id(0)
    first = (i % tiles_per_seq) == 0
    seg = tm // SUBLANES
    pitch = seg + 1
    base = CONV_HALO - (CONV_WIDTH - 1)
    assert CONV_HALO < seg and seg % CONV_HALO == 0

    def gap(nrow):
        return nrow + nrow // seg

    def lane_col(c, carry):
        l0 = pl.multiple_of(c * LANES, LANES)
        lanes = pl.ds(l0, LANES)
        slab_ref[0:CONV_HALO, :] = jnp.where(first, 0.0, prev_ref[:, lanes].astype(F32))
        slab_ref[CONV_HALO:seg, :] = u_ref[0:seg - CONV_HALO, lanes].astype(F32)
        for q in range(1, SUBLANES):
            slab_ref[q * pitch:q * pitch + seg, :] = (
                u_ref[q * seg - CONV_HALO:(q + 1) * seg - CONV_HALO, lanes].astype(F32))
        slab_ref[SUBLANES * pitch:SUBLANES * pitch + CONV_HALO, :] = (
            u_ref[tm - CONV_HALO:tm, lanes].astype(F32))
        for k0 in range(0, seg, CONV_KB):
            data = [slab_ref[pl.ds(gap(k0 + base + m), SUBLANES, stride=pitch), :]
                    for m in range(CONV_KB + CONV_WIDTH - 1)]
            accs = [jnp.zeros((SUBLANES, LANES), F32) for _ in range(CONV_KB)]
            for j in range(CONV_WIDTH):
                wj = w_ref[j, :, lanes]
                for kk in range(CONV_KB):
                    accs[kk] = accs[kk] + wj * data[kk + j]
            for kk in range(CONV_KB):
                acc_ref[c, pl.ds(k0 + kk, SUBLANES, stride=seg), :] = accs[kk]
        return carry

    lax.fori_loop(0, D_MODEL // LANES, lane_col, 0)

    def row_block(rb, carry):
        r0 = pl.multiple_of(rb * CONV_RB, CONV_RB)
        rows = pl.ds(r0, CONV_RB)
        y = acc_ref[:, rows, :] + bdw_ref[...]
        mu = jnp.sum(jnp.sum(y, axis=0), axis=-1, keepdims=True) * (1.0 / D_MODEL)
        yc = y - mu
        var = jnp.sum(jnp.sum(yc * yc, axis=0), axis=-1, keepdims=True) * (1.0 / D_MODEL)
        yn = yc * lax.rsqrt(var + EPS) * gln_ref[...] + bln_ref[...]
        for c in range(D_MODEL // LANES):
            sl = slice(c * LANES, (c + 1) * LANES)
            o_ref[rows, sl] = (_silu(yn[c]) * z_ref[rows, sl].astype(F32)).astype(o_ref.dtype)
        return carry

    lax.fori_loop(0, tm // CONV_RB, row_block, 0, unroll=8)


def _conv_branch(u, z, z_col, w_dw_b, b_dw, g_ln, b_ln, tm, seq):
    n, d = u.shape
    tiles_per_seq = seq // tm
    hb = tm // CONV_HALO
    vec = pl.BlockSpec((d // LANES, 1, LANES), lambda i: (0, 0, 0))
    return pl.pallas_call(
        functools.partial(_conv_kernel, tm=tm, tiles_per_seq=tiles_per_seq),
        out_shape=jax.ShapeDtypeStruct((n, d), BF16),
        grid=(n // tm,),
        in_specs=[pl.BlockSpec((tm, d), lambda i: (i, 0)),
                  pl.BlockSpec((CONV_HALO, d), lambda i: (jnp.maximum(i * hb - 1, 0), 0)),
                  pl.BlockSpec((tm, d), lambda i: (i, z_col)),
                  pl.BlockSpec((CONV_WIDTH, SUBLANES, d), lambda i: (0, 0, 0)),
                  vec, vec, vec],
        out_specs=pl.BlockSpec((tm, d), lambda i: (i, 0)),
        scratch_shapes=[pltpu.VMEM((tm + CONV_HALO + SUBLANES, LANES), F32),
                        pltpu.VMEM((d // LANES, tm, LANES), F32)],
        compiler_params=_params("parallel"),
        name="conv_branch",
    )(u, u, z, w_dw_b, b_dw, g_ln, b_ln)


ML_CHUNK = 256
QK_HALO = 16


def _log_sigmoid(t):
    return jnp.minimum(t, 0.0) - jnp.log(1.0 + jnp.exp(-jnp.abs(t)))


def _mlstm_kernel(q_ref, k_ref, qprev_ref, kprev_ref, v_ref, cw_ref, perm_ref, gc_ref, gr_ref,
                  o_ref, z_ref, gh_ref, out_ref, c_ref, n_ref, m_ref, slab_ref, qp_ref, kp_ref):
    first = pl.program_id(1) == 0
    L = ML_CHUNK
    seg = L // SUBLANES
    pitch = seg + 1
    base = QK_HALO - (QK_CONV_WIDTH - 1)
    assert QK_HALO < seg and seg % QK_HALO == 0

    @pl.when(first)
    def _():
        c_ref[...] = jnp.zeros_like(c_ref)
        n_ref[...] = jnp.zeros_like(n_ref)
        m_ref[...] = jnp.zeros_like(m_ref)

    def gap(nrow):
        return nrow + nrow // seg

    def conv_col(lc, carry):
        l0 = pl.multiple_of(lc * LANES, LANES)
        lanes = pl.ds(l0, LANES)
        for a, (pre_ref, prev_ref, dst_ref, scale) in enumerate((
                (q_ref, qprev_ref, qp_ref, 1.0), (k_ref, kprev_ref, kp_ref, HEAD_DIM ** -0.5))):
            slab_ref[a, 0:QK_HALO, :] = jnp.where(first, 0.0, prev_ref[:, lanes].astype(F32))
            slab_ref[a, QK_HALO:seg, :] = pre_ref[0:seg - QK_HALO, lanes].astype(F32)
            for s in range(1, SUBLANES):
                slab_ref[a, s * pitch:s * pitch + seg, :] = (
                    pre_ref[s * seg - QK_HALO:(s + 1) * seg - QK_HALO, lanes].astype(F32))
            slab_ref[a, SUBLANES * pitch:SUBLANES * pitch + QK_HALO, :] = (
                pre_ref[L - QK_HALO:L, lanes].astype(F32))
            data = [slab_ref[a, pl.ds(gap(base + m), SUBLANES, stride=pitch), :]
                    for m in range(seg + QK_CONV_WIDTH - 1)]
            wl = pl.ds(pl.multiple_of(a * D_MODEL + l0, LANES), LANES)
            w = [cw_ref[j, :, wl] for j in range(QK_CONV_WIDTH)]
            outs = []
            for kk in range(seg):
                acc = w[0] * data[kk]
                for j in range(1, QK_CONV_WIDTH):
                    acc = acc + w[j] * data[kk + j]
                outs.append(_silu(acc) * scale)
            for kk in range(0, seg, 2):
                dst_ref[kk * SUBLANES:(kk + 2) * SUBLANES, lanes] = (
                    jnp.concatenate(outs[kk:kk + 2], axis=0).astype(BF16))
        return carry

    lax.fori_loop(0, D_MODEL // LANES, conv_col, 0)

    perm = perm_ref[...]
    ri = lax.broadcasted_iota(jnp.int32, (L, L), 0)
    ci = lax.broadcasted_iota(jnp.int32, (L, L), 1)
    causal = ri >= ci
    upper = ri <= ci
    gcol = gc_ref[...]
    lane = lax.broadcasted_iota(jnp.int32, gcol.shape, 1)
    grow = gr_ref[0]
    sub = lax.broadcasted_iota(jnp.int32, grow.shape, 0)

    for h in range(N_HEADS):
        hs = slice(h * HEAD_DIM, (h + 1) * HEAD_DIM)
        q = jnp.dot(perm, qp_ref[:, hs], preferred_element_type=F32)
        k = jnp.dot(perm, kp_ref[:, hs], preferred_element_type=F32)
        v = v_ref[:, hs]
        qb = q.astype(BF16)

        li_col = jnp.sum(jnp.where(lane == h, gcol, 0.0), axis=-1, keepdims=True)
        f_col = jnp.sum(jnp.where(lane == h + N_HEADS, gcol, 0.0), axis=-1, keepdims=True)
        li_row = jnp.sum(jnp.where(sub == h, grow, 0.0), axis=0, keepdims=True)
        f_row = jnp.sum(jnp.where(sub == h + N_HEADS, grow, 0.0), axis=0, keepdims=True)
        lf_col = _log_sigmoid(f_col)
        lf_row = _log_sigmoid(f_row)
        b_col = jnp.sum(jnp.where(causal, lf_row, 0.0), axis=-1, keepdims=True)
        b_row = jnp.sum(jnp.where(upper, lf_col, 0.0), axis=0, keepdims=True)

        m_prev = m_ref[h, 0:1, 0:1]
        d = jnp.where(causal, b_col - b_row + li_row, NEG_BIG)
        inter = b_col + m_prev
        m_row = jnp.maximum(inter, jnp.max(d, axis=-1, keepdims=True))
        w_intra = jnp.exp(d - m_row)
        w_inter = jnp.exp(inter - m_row)

        s = lax.dot_general(qb, k.astype(BF16), (((1,), (1,)), ((), ())),
                            preferred_element_type=F32) * w_intra
        num = (jnp.dot(s.astype(BF16), v, preferred_element_type=F32)
               + w_inter * jnp.dot(qb, c_ref[h].astype(BF16), preferred_element_type=F32))
        den = (jnp.sum(s, axis=-1, keepdims=True)
               + w_inter * jnp.sum(q * n_ref[h], axis=-1, keepdims=True))
        hval = num / jnp.maximum(jnp.abs(den), jnp.exp(-m_row))

        b_last = b_col[L - 1:L, :]
        g_row = b_last - b_row + li_row
        g_col = b_last - b_col + li_col
        m_new = jnp.maximum(b_last + m_prev, jnp.max(g_row, axis=-1, keepdims=True))
        decay = jnp.exp(b_last + m_prev - m_new)
        kw = k * jnp.exp(g_col - m_new)
        c_ref[h] = decay * c_ref[h] + lax.dot_general(
            kw.astype(BF16), v, (((0,), (0,)), ((), ())), preferred_element_type=F32)
        n_ref[h] = decay * n_ref[h] + jnp.sum(kw, axis=0, keepdims=True)
        m_ref[h] = jnp.broadcast_to(m_new, m_ref.shape[1:])

        hm = o_ref[:, hs].astype(F32) * hval
        hm = hm * lax.rsqrt(jnp.mean(hm * hm, axis=-1, keepdims=True) + EPS) * gh_ref[:, hs]
        out_ref[:, hs] = (hm * z_ref[:, hs].astype(F32)).astype(out_ref.dtype)


def _mlstm_branch(qkv, o_gate, z_gate, cw_b, perm, gif, gif_t, g_head, batch, seq):
    n = qkv.shape[0]
    L = ML_CHUNK
    nc = seq // L
    hb = L // QK_HALO

    def tok(col_group):
        return pl.BlockSpec((L, D_MODEL), lambda b, c: (b * nc + c, col_group))

    def prev(col_group):
        return pl.BlockSpec((QK_HALO, D_MODEL),
                            lambda b, c: (jnp.maximum((b * nc + c) * hb - 1, 0), col_group))

    return pl.pallas_call(
        _mlstm_kernel,
        out_shape=jax.ShapeDtypeStruct((n, D_MODEL), BF16),
        grid=(batch, nc),
        in_specs=[tok(0), tok(1), prev(0), prev(1), tok(2),
                  pl.BlockSpec((QK_CONV_WIDTH, SUBLANES, 2 * D_MODEL), lambda b, c: (0, 0, 0)),
                  pl.BlockSpec((L, L), lambda b, c: (0, 0)),
                  pl.BlockSpec((L, LANES), lambda b, c: (b * nc + c, 0)),
                  pl.BlockSpec((1, SUBLANES, L), lambda b, c: (b, 0, c)),
                  tok(0), tok(0),
                  pl.BlockSpec((1, D_MODEL), lambda b, c: (0, 0))],
        out_specs=pl.BlockSpec((L, D_MODEL), lambda b, c: (b * nc + c, 0)),
        scratch_shapes=[pltpu.VMEM((N_HEADS, HEAD_DIM, HEAD_DIM), F32),
                        pltpu.VMEM((N_HEADS, 1, HEAD_DIM), F32),
                        pltpu.VMEM((N_HEADS, SUBLANES, LANES), F32),
                        pltpu.VMEM((2, L + QK_HALO + SUBLANES, LANES), F32),
                        pltpu.VMEM((L, D_MODEL), BF16),
                        pltpu.VMEM((L, D_MODEL), BF16)],
        compiler_params=_params("parallel", "arbitrary"),
        name="mlstm",
    )(qkv, qkv, qkv, qkv, qkv, cw_b, perm, gif, gif_t, o_gate, z_gate, g_head)


def _xattn_kernel(q_ref, k_ref, v_ref, z_ref, o_ref):
    scale = HEAD_DIM ** -0.5
    for h in range(N_HEADS):
        sl = slice(h * HEAD_DIM, (h + 1) * HEAD_DIM)
        s = lax.dot_general(q_ref[:, sl], k_ref[:, sl], (((1,), (1,)), ((), ())),
                            preferred_element_type=F32) * scale
        e = jnp.exp(s - jnp.max(s, axis=-1, keepdims=True))
        p = e / jnp.sum(e, axis=-1, keepdims=True)
        o = jnp.dot(p.astype(BF16), v_ref[:, sl], preferred_element_type=F32)
        o_ref[:, sl] = (o * z_ref[:, sl].astype(F32)).astype(o_ref.dtype)


def _xattn_branch(q, z_gate, kv, mem_len, tm, seq):
    n = q.shape[0]
    tiles_per_seq = seq // tm
    return pl.pallas_call(
        _xattn_kernel,
        out_shape=jax.ShapeDtypeStruct((n, D_MODEL), BF16),
        grid=(n // tm,),
        in_specs=[pl.BlockSpec((tm, D_MODEL), lambda i: (i, 0)),
                  pl.BlockSpec((mem_len, D_MODEL), lambda i: (i // tiles_per_seq, 0)),
                  pl.BlockSpec((mem_len, D_MODEL), lambda i: (i // tiles_per_seq, 1)),
                  pl.BlockSpec((tm, D_MODEL), lambda i: (i, 0))],
        out_specs=pl.BlockSpec((tm, D_MODEL), lambda i: (i, 0)),
        compiler_params=_params("parallel"),
        name="xattn",
    )(q, kv, kv, z_gate)


def _merge_kernel(ac_ref, am_ref, ax_ref, wc_ref, wm_ref, wx_ref, gc_ref, gm_ref, gx_ref, o_ref):
    acc = gc_ref[...].astype(F32) * jnp.dot(ac_ref[...], wc_ref[...], preferred_element_type=F32)
    acc = acc + gm_ref[...].astype(F32) * jnp.dot(am_ref[...], wm_ref[...], preferred_element_type=F32)
    acc = acc + gx_ref[...].astype(F32) * jnp.dot(ax_ref[...], wx_ref[...], preferred_element_type=F32)
    o_ref[...] = acc.astype(o_ref.dtype)


def _merge(a_c, a_m, a_x, w_c, w_m, w_x, gates, gate_col, tm, tn):
    n, d = a_c.shape
    ncol = D_MODEL // tn
    act = pl.BlockSpec((tm, d), lambda j, i: (i, 0))
    wsp = pl.BlockSpec((d, tn), lambda j, i: (0, j))

    def gate(k):
        return pl.BlockSpec((tm, tn), lambda j, i: (i, (gate_col + k) * ncol + j))

    return pl.pallas_call(
        _merge_kernel,
        out_shape=jax.ShapeDtypeStruct((n, D_MODEL), BF16),
        grid=(ncol, n // tm),
        in_specs=[act, act, act, wsp, wsp, wsp, gate(0), gate(1), gate(2)],
        out_specs=pl.BlockSpec((tm, tn), lambda j, i: (i, j)),
        compiler_params=_params("parallel", "parallel"),
        name="merge",
    )(a_c, a_m, a_x, w_c, w_m, w_x, gates, gates, gates)


def _final_kernel(m_ref, w_ref, g_ref, x_ref, o_ref):
    y = jnp.dot(m_ref[...], w_ref[...], preferred_element_type=F32)
    y = y * lax.rsqrt(jnp.mean(y * y, axis=-1, keepdims=True) + EPS) * g_ref[...]
    o_ref[...] = x_ref[...] + y


def _final(merged, w_out, g_post, x2, tm):
    n, d = x2.shape
    return pl.pallas_call(
        _final_kernel,
        out_shape=jax.ShapeDtypeStruct((n, d), F32),
        grid=(n // tm,),
        in_specs=[pl.BlockSpec((tm, d), lambda i: (i, 0)),
                  pl.BlockSpec((d, d), lambda i: (0, 0)),
                  pl.BlockSpec((1, d), lambda i: (0, 0)),
                  pl.BlockSpec((tm, d), lambda i: (i, 0))],
        out_specs=pl.BlockSpec((tm, d), lambda i: (i, 0)),
        compiler_params=_params("parallel"),
        name="final",
    )(merged, w_out, g_post, x2)


def kernel(x, mem, g_pre, w_in, b_if, w_qk_conv, w_dw, b_dw, g_ln, b_ln, w_conv_out, g_ml_head,
           w_ml_out, g_mem, w_mem_kv, w_xa_out, w_out, g_post):
    batch, seq, d = x.shape
    mem_len = mem.shape[1]
    n = batch * seq
    assert d == D_MODEL and seq % ML_CHUNK == 0

    nif = 2 * N_HEADS
    if0 = 8 * D_MODEL
    w_in_t = w_in.T
    w_head = w_in_t[:if0].astype(BF16)
    w_tail = w_in_t[if0 + nif:].astype(BF16)

    w_if = jnp.pad(w_in[:, 8 * D_MODEL:8 * D_MODEL + nif], ((0, 0), (0, LANES - nif)))
    w_if_hi = w_if.astype(BF16)
    w_if_lo = (w_if - w_if_hi.astype(F32)).astype(BF16)
    w_if_hl = jnp.concatenate([w_if_hi, w_if_lo], axis=1)
    b_if_pad = jnp.pad(b_if, (0, LANES - nif)).reshape(1, LANES)

    def row(vec):
        return vec.reshape(1, -1).astype(F32)

    def lane_cols(vec):
        return vec.reshape(d // LANES, 1, LANES).astype(F32)

    x2 = x.reshape(n, d)
    h, gif, gif_t = _prenorm(x2, row(g_pre), w_if_hl, b_if_pad, tm=1024, batch=batch)

    def proj(w_t, group, ngroups, act):
        return _proj(h, w_t, group * D_MODEL, ngroups * D_MODEL, act, tm=1024, tn=2048)

    u = _glu(h, w_head, tm=1024, tn=1024)
    z_conv = proj(w_head, 2, 1, _silu_t)
    qkv = proj(w_head, 3, 3, _identity)
    o_gate = proj(w_head, 6, 1, _sigmoid_t)
    z_ml = proj(w_head, 7, 1, _silu_t)
    q_xa = proj(w_tail, 0, 1, _identity)
    z_xa = proj(w_tail, 1, 1, _silu_t)
    gates = proj(w_tail, 2, 3, _sigmoid_t)

    w_dw_b = jnp.broadcast_to(w_dw.astype(F32)[:, None, :], (CONV_WIDTH, SUBLANES, d))
    a_c = _conv_branch(u, z_conv, 0, w_dw_b, lane_cols(b_dw), lane_cols(g_ln),
                       lane_cols(b_ln), tm=512, seq=seq)

    cw_b = jnp.broadcast_to(w_qk_conv.astype(F32)[:, None, :], (QK_CONV_WIDTH, SUBLANES, 2 * d))
    p_idx = jnp.arange(ML_CHUNK)
    t_of_p = (p_idx % SUBLANES) * (ML_CHUNK // SUBLANES) + p_idx // SUBLANES
    perm = (jnp.arange(ML_CHUNK)[:, None] == t_of_p[None, :]).astype(BF16)
    a_m = _mlstm_branch(qkv, o_gate, z_ml, cw_b, perm, gif, gif_t, row(g_ml_head), batch, seq)

    mem_h = _memnorm(mem.reshape(batch * mem_len, d), row(g_mem), tm=256)
    kv = _proj(mem_h, w_mem_kv.astype(BF16), 0, 2 * D_MODEL, _identity, tm=256, tn=2048,
               out_major=False)
    a_x = _xattn_branch(q_xa, z_xa, kv, mem_len, tm=1024, seq=seq)

    merged = _merge(a_c, a_m, a_x, w_conv_out.astype(BF16), w_ml_out.astype(BF16),
                    w_xa_out.astype(BF16), gates, 0, tm=512, tn=1024)
    out = _final(merged, w_out.astype(BF16), row(g_post), x2, tm=512)
    return out.reshape(batch, seq, d)
```

```python
import functools

import jax
import jax.numpy as jnp
from jax import lax
from jax.experimental import pallas as pl
from jax.experimental.pallas import tpu as pltpu

F32 = jnp.float32
BF16 = jnp.bfloat16

D_MODEL = 2048
N_HEADS = 4
HEAD_DIM = D_MODEL // N_HEADS
CONV_WIDTH = 31
QK_CONV_WIDTH = 4
EPS = 1e-6
NEG_BIG = -1e30

LANES = 128
SUBLANES = 8
VMEM_LIMIT = 56 * 1024 * 1024


def _sigmoid(y):
    return 1.0 / (1.0 + jnp.exp(-y))


def _silu(y):
    return y * _sigmoid(y)


def _sigmoid_t(y):
    return 0.5 * jnp.tanh(0.5 * y) + 0.5


def _silu_t(y):
    t = 0.5 * y
    return t * (jnp.tanh(t) + 1.0)


def _identity(y):
    return y


def _params(*sem):
    return pltpu.CompilerParams(dimension_semantics=sem, vmem_limit_bytes=VMEM_LIMIT)


def _prenorm_kernel(x_ref, g_ref, whl_ref, bif_ref, h_ref, gif_ref, gift_ref):
    xf = x_ref[...]
    y = xf * lax.rsqrt(jnp.mean(xf * xf, axis=-1, keepdims=True) + EPS) * g_ref[...]
    hi = y.astype(BF16)
    h_ref[...] = hi
    lo = (y - hi.astype(F32)).astype(BF16)
    hh = jnp.dot(hi, whl_ref[...], preferred_element_type=F32)
    lh = jnp.dot(lo, whl_ref[:, 0:LANES], preferred_element_type=F32)
    gif = hh[:, 0:LANES] + (hh[:, LANES:] + lh) + bif_ref[...]
    gif_ref[...] = gif
    gift_ref[0] = gif.T[0:SUBLANES, :]


def _prenorm(x2, g, w_if_hl, bif, tm, batch):
    n, d = x2.shape
    tiles_per_seq = n // batch // tm
    return pl.pallas_call(
        _prenorm_kernel,
        out_shape=(jax.ShapeDtypeStruct((n, d), BF16), jax.ShapeDtypeStruct((n, LANES), F32),
                   jax.ShapeDtypeStruct((batch, SUBLANES, n // batch), F32)),
        grid=(n // tm,),
        in_specs=[pl.BlockSpec((tm, d), lambda i: (i, 0)),
                  pl.BlockSpec((1, d), lambda i: (0, 0)),
                  pl.BlockSpec((d, 2 * LANES), lambda i: (0, 0)),
                  pl.BlockSpec((1, LANES), lambda i: (0, 0))],
        out_specs=(pl.BlockSpec((tm, d), lambda i: (i, 0)),
                   pl.BlockSpec((tm, LANES), lambda i: (i, 0)),
                   pl.BlockSpec((1, SUBLANES, tm),
                                lambda i: (i // tiles_per_seq, 0, i % tiles_per_seq))),
        compiler_params=_params("parallel"),
        name="prenorm",
    )(x2, g, w_if_hl, bif)


def _memnorm_kernel(x_ref, g_ref, h_ref):
    xf = x_ref[...]
    y = xf * lax.rsqrt(jnp.mean(xf * xf, axis=-1, keepdims=True) + EPS) * g_ref[...]
    h_ref[...] = y.astype(BF16)


def _memnorm(x2, g, tm):
    n, d = x2.shape
    return pl.pallas_call(
        _memnorm_kernel,
        out_shape=jax.ShapeDtypeStruct((n, d), BF16),
        grid=(n // tm,),
        in_specs=[pl.BlockSpec((tm, d), lambda i: (i, 0)),
                  pl.BlockSpec((1, d), lambda i: (0, 0))],
        out_specs=pl.BlockSpec((tm, d), lambda i: (i, 0)),
        compiler_params=_params("parallel"),
        name="memnorm",
    )(x2, g)


PROJ_SUB = 512


def _dot_t(a, w_t):
    return lax.dot_general(a, w_t, (((1,), (1,)), ((), ())), preferred_element_type=F32)


def _proj_kernel(h_ref, w_ref, o_ref, *, act, out_major):
    hh = h_ref[...]
    for c0 in range(0, o_ref.shape[1], PROJ_SUB):
        sl = slice(c0, c0 + PROJ_SUB)
        y = (_dot_t(hh, w_ref[sl, :]) if out_major
             else jnp.dot(hh, w_ref[:, sl], preferred_element_type=F32))
        o_ref[:, sl] = act(y).astype(o_ref.dtype)


def _proj(h, w, start, ncols, act, tm, tn, out_major=True):
    n, d = h.shape
    off = start // tn
    wspec = (pl.BlockSpec((tn, d), lambda j, i: (j + off, 0)) if out_major
             else pl.BlockSpec((d, tn), lambda j, i: (0, j + off)))
    return pl.pallas_call(
        functools.partial(_proj_kernel, act=act, out_major=out_major),
        out_shape=jax.ShapeDtypeStruct((n, ncols), BF16),
        grid=(ncols // tn, n // tm),
        in_specs=[pl.BlockSpec((tm, d), lambda j, i: (i, 0)), wspec],
        out_specs=pl.BlockSpec((tm, tn), lambda j, i: (i, j)),
        compiler_params=_params("parallel", "parallel"),
        name="proj",
    )(h, w)


def _glu_kernel(h_ref, wa_ref, wb_ref, o_ref):
    hh = h_ref[...]
    a = _dot_t(hh, wa_ref[...])
    b = _dot_t(hh, wb_ref[...])
    o_ref[...] = (a * _sigmoid_t(b)).astype(o_ref.dtype)


def _glu(h, w_t, tm, tn):
    n, d = h.shape
    ncol = D_MODEL // tn
    return pl.pallas_call(
        _glu_kernel,
        out_shape=jax.ShapeDtypeStruct((n, D_MODEL), BF16),
        grid=(ncol, n // tm),
        in_specs=[pl.BlockSpec((tm, d), lambda j, i: (i, 0)),
                  pl.BlockSpec((tn, d), lambda j, i: (j, 0)),
                  pl.BlockSpec((tn, d), lambda j, i: (j + ncol, 0))],
        out_specs=pl.BlockSpec((tm, tn), lambda j, i: (i, j)),
        compiler_params=_params("parallel", "parallel"),
        name="proj_glu",
    )(h, w_t, w_t)


CONV_HALO = 32
CONV_KB = 8
CONV_RB = 16


def _conv_kernel(u_ref, prev_ref, z_ref, w_ref, bdw_ref, gln_ref, bln_ref, o_ref,
                 slab_ref, acc_ref, *, tm, tiles_per_seq):
    i = pl.program_id(0)
    first = (i % tiles_per_seq) == 0
    seg = tm // SUBLANES
    pitch = seg + 1
    base = CONV_HALO - (CONV_WIDTH - 1)
    assert CONV_HALO < seg and seg % CONV_HALO == 0

    def gap(nrow):
        return nrow + nrow // seg

    def lane_col(c, carry):
        l0 = pl.multiple_of(c * LANES, LANES)
        lanes = pl.ds(l0, LANES)
        slab_ref[0:CONV_HALO, :] = jnp.where(first, 0.0, prev_ref[:, lanes].astype(F32))
        slab_ref[CONV_HALO:seg, :] = u_ref[0:seg - CONV_HALO, lanes].astype(F32)
        for q in range(1, SUBLANES):
            slab_ref[q * pitch:q * pitch + seg, :] = (
                u_ref[q * seg - CONV_HALO:(q + 1) * seg - CONV_HALO, lanes].astype(F32))
        slab_ref[SUBLANES * pitch:SUBLANES * pitch + CONV_HALO, :] = (
            u_ref[tm - CONV_HALO:tm, lanes].astype(F32))
        for k0 in range(0, seg, CONV_KB):
            data = [slab_ref[pl.ds(gap(k0 + base + m), SUBLANES, stride=pitch), :]
                    for m in range(CONV_KB + CONV_WIDTH - 1)]
            accs = [jnp.zeros((SUBLANES, LANES), F32) for _ in range(CONV_KB)]
            for j in range(CONV_WIDTH):
                wj = w_ref[j, :, lanes]
                for kk in range(CONV_KB):
                    accs[kk] = accs[kk] + wj * data[kk + j]
            for kk in range(CONV_KB):
                acc_ref[c, pl.ds(k0 + kk, SUBLANES, stride=seg), :] = accs[kk]
        return carry

    lax.fori_loop(0, D_MODEL // LANES, lane_col, 0)

    def row_block(rb, carry):
        r0 = pl.multiple_of(rb * CONV_RB, CONV_RB)
        rows = pl.ds(r0, CONV_RB)
        y = acc_ref[:, rows, :] + bdw_ref[...]
        mu = jnp.sum(jnp.sum(y, axis=0), axis=-1, keepdims=True) * (1.0 / D_MODEL)
        yc = y - mu
        var = jnp.sum(jnp.sum(yc * yc, axis=0), axis=-1, keepdims=True) * (1.0 / D_MODEL)
        yn = yc * lax.rsqrt(var + EPS) * gln_ref[...] + bln_ref[...]
        for c in range(D_MODEL // LANES):
            sl = slice(c * LANES, (c + 1) * LANES)
            o_ref[rows, sl] = (_silu_t(yn[c]) * z_ref[rows, sl].astype(F32)).astype(o_ref.dtype)
        return carry

    lax.fori_loop(0, tm // CONV_RB, row_block, 0, unroll=8)


def _conv_branch(u, z, z_col, w_dw_b, b_dw, g_ln, b_ln, tm, seq):
    n, d = u.shape
    tiles_per_seq = seq // tm
    hb = tm // CONV_HALO
    vec = pl.BlockSpec((d // LANES, 1, LANES), lambda i: (0, 0, 0))
    return pl.pallas_call(
        functools.partial(_conv_kernel, tm=tm, tiles_per_seq=tiles_per_seq),
        out_shape=jax.ShapeDtypeStruct((n, d), BF16),
        grid=(n // tm,),
        in_specs=[pl.BlockSpec((tm, d), lambda i: (i, 0)),
                  pl.BlockSpec((CONV_HALO, d), lambda i: (jnp.maximum(i * hb - 1, 0), 0)),
                  pl.BlockSpec((tm, d), lambda i: (i, z_col)),
                  pl.BlockSpec((CONV_WIDTH, SUBLANES, d), lambda i: (0, 0, 0)),
                  vec, vec, vec],
        out_specs=pl.BlockSpec((tm, d), lambda i: (i, 0)),
        scratch_shapes=[pltpu.VMEM((tm + CONV_HALO + SUBLANES, LANES), F32),
                        pltpu.VMEM((d // LANES, tm, LANES), F32)],
        compiler_params=_params("parallel"),
        name="conv_branch",
    )(u, u, z, w_dw_b, b_dw, g_ln, b_ln)


ML_CHUNK = 256
QK_HALO = 16


def _log_sigmoid(t):
    return jnp.minimum(t, 0.0) - jnp.log(1.0 + jnp.exp(-jnp.abs(t)))


def _mlstm_kernel(q_ref, k_ref, qprev_ref, kprev_ref, v_ref, cw_ref, shift_ref, gc_ref, gr_ref,
                  o_ref, z_ref, gh_ref, out_ref, c_ref, n_ref, m_ref):
    first = pl.program_id(1) == 0
    L = ML_CHUNK

    @pl.when(first)
    def _():
        c_ref[...] = jnp.zeros_like(c_ref)
        n_ref[...] = jnp.zeros_like(n_ref)
        m_ref[...] = jnp.zeros_like(m_ref)

    def conv4(pre_ref, prev_ref, a):
        pre = pre_ref[...]
        w = cw_ref[a]
        acc = w[QK_CONV_WIDTH - 1:QK_CONV_WIDTH, :] * pre.astype(F32)
        for s in range(1, QK_CONV_WIDTH):
            sh = jnp.dot(shift_ref[s - 1], pre, preferred_element_type=F32)
            acc = acc + w[QK_CONV_WIDTH - 1 - s:QK_CONV_WIDTH - s, :] * sh
        tail = jnp.where(first, 0.0, prev_ref[QK_HALO - SUBLANES:, :].astype(F32))
        rid = lax.broadcasted_iota(jnp.int32, tail.shape, 0)
        fix = jnp.zeros_like(tail)
        for s in range(1, QK_CONV_WIDTH):
            rolled = pltpu.roll(tail, s, axis=0)
            fix = fix + w[QK_CONV_WIDTH - 1 - s:QK_CONV_WIDTH - s, :] * jnp.where(rid < s, rolled, 0.0)
        acc = jnp.concatenate([acc[0:SUBLANES, :] + fix, acc[SUBLANES:, :]], axis=0)
        return _silu_t(acc)

    q_all = conv4(q_ref, qprev_ref, 0)
    k_all = conv4(k_ref, kprev_ref, 1) * (HEAD_DIM ** -0.5)

    ri = lax.broadcasted_iota(jnp.int32, (L, L), 0)
    ci = lax.broadcasted_iota(jnp.int32, (L, L), 1)
    causal = ri >= ci
    upper = ri <= ci
    gcol = gc_ref[...]
    lane = lax.broadcasted_iota(jnp.int32, gcol.shape, 1)
    grow = gr_ref[0]
    sub = lax.broadcasted_iota(jnp.int32, grow.shape, 0)

    for h in range(N_HEADS):
        hs = slice(h * HEAD_DIM, (h + 1) * HEAD_DIM)
        q = q_all[:, hs]
        k = k_all[:, hs]
        v = v_ref[:, hs]
        qb = q.astype(BF16)

        li_col = jnp.sum(jnp.where(lane == h, gcol, 0.0), axis=-1, keepdims=True)
        f_col = jnp.sum(jnp.where(lane == h + N_HEADS, gcol, 0.0), axis=-1, keepdims=True)
        li_row = jnp.sum(jnp.where(sub == h, grow, 0.0), axis=0, keepdims=True)
        f_row = jnp.sum(jnp.where(sub == h + N_HEADS, grow, 0.0), axis=0, keepdims=True)
        lf_col = _log_sigmoid(f_col)
        lf_row = _log_sigmoid(f_row)
        b_col = jnp.sum(jnp.where(causal, lf_row, 0.0), axis=-1, keepdims=True)
        b_row = jnp.sum(jnp.where(upper, lf_col, 0.0), axis=0, keepdims=True)

        m_prev = m_ref[h, 0:1, 0:1]
        d = jnp.where(causal, b_col - b_row + li_row, NEG_BIG)
        inter = b_col + m_prev
        m_row = jnp.maximum(inter, jnp.max(d, axis=-1, keepdims=True))
        w_intra = jnp.exp(d - m_row)
        w_inter = jnp.exp(inter - m_row)

        s = lax.dot_general(qb, k.astype(BF16), (((1,), (1,)), ((), ())),
                            preferred_element_type=F32) * w_intra
        num = (jnp.dot(s.astype(BF16), v, preferred_element_type=F32)
               + w_inter * jnp.dot(qb, c_ref[h].astype(BF16), preferred_element_type=F32))
        den = (jnp.sum(s, axis=-1, keepdims=True)
               + w_inter * jnp.sum(q * n_ref[h], axis=-1, keepdims=True))
        hval = num / jnp.maximum(jnp.abs(den), jnp.exp(-m_row))

        b_last = b_col[L - 1:L, :]
        g_row = b_last - b_row + li_row
        g_col = b_last - b_col + li_col
        m_new = jnp.maximum(b_last + m_prev, jnp.max(g_row, axis=-1, keepdims=True))
        decay = jnp.exp(b_last + m_prev - m_new)
        kw = k * jnp.exp(g_col - m_new)
        c_ref[h] = decay * c_ref[h] + lax.dot_general(
            kw.astype(BF16), v, (((0,), (0,)), ((), ())), preferred_element_type=F32)
        n_ref[h] = decay * n_ref[h] + jnp.sum(kw, axis=0, keepdims=True)
        m_ref[h] = jnp.broadcast_to(m_new, m_ref.shape[1:])

        hm = o_ref[:, hs].astype(F32) * hval
        hm = hm * lax.rsqrt(jnp.mean(hm * hm, axis=-1, keepdims=True) + EPS) * gh_ref[:, hs]
        out_ref[:, hs] = (hm * z_ref[:, hs].astype(F32)).astype(out_ref.dtype)


def _mlstm_branch(qkv, o_gate, z_gate, cw, shifts, gif, gif_t, g_head, batch, seq):
    n = qkv.shape[0]
    L = ML_CHUNK
    nc = seq // L
    hb = L // QK_HALO

    def tok(col_group):
        return pl.BlockSpec((L, D_MODEL), lambda b, c: (b * nc + c, col_group))

    def prev(col_group):
        return pl.BlockSpec((QK_HALO, D_MODEL),
                            lambda b, c: (jnp.maximum((b * nc + c) * hb - 1, 0), col_group))

    return pl.pallas_call(
        _mlstm_kernel,
        out_shape=jax.ShapeDtypeStruct((n, D_MODEL), BF16),
        grid=(batch, nc),
        in_specs=[tok(0), tok(1), prev(0), prev(1), tok(2),
                  pl.BlockSpec((2, QK_CONV_WIDTH, D_MODEL), lambda b, c: (0, 0, 0)),
                  pl.BlockSpec((QK_CONV_WIDTH - 1, L, L), lambda b, c: (0, 0, 0)),
                  pl.BlockSpec((L, LANES), lambda b, c: (b * nc + c, 0)),
                  pl.BlockSpec((1, SUBLANES, L), lambda b, c: (b, 0, c)),
                  tok(0), tok(0),
                  pl.BlockSpec((1, D_MODEL), lambda b, c: (0, 0))],
        out_specs=pl.BlockSpec((L, D_MODEL), lambda b, c: (b * nc + c, 0)),
        scratch_shapes=[pltpu.VMEM((N_HEADS, HEAD_DIM, HEAD_DIM), F32),
                        pltpu.VMEM((N_HEADS, 1, HEAD_DIM), F32),
                        pltpu.VMEM((N_HEADS, SUBLANES, LANES), F32)],
        compiler_params=_params("parallel", "arbitrary"),
        name="mlstm",
    )(qkv, qkv, qkv, qkv, qkv, cw, shifts, gif, gif_t, o_gate, z_gate, g_head)


def _xattn_kernel(q_ref, k_ref, v_ref, z_ref, o_ref):
    scale = HEAD_DIM ** -0.5
    for h in range(N_HEADS):
        sl = slice(h * HEAD_DIM, (h + 1) * HEAD_DIM)
        s = lax.dot_general(q_ref[:, sl], k_ref[:, sl], (((1,), (1,)), ((), ())),
                            preferred_element_type=F32) * scale
        e = jnp.exp(s - jnp.max(s, axis=-1, keepdims=True))
        p = e / jnp.sum(e, axis=-1, keepdims=True)
        o = jnp.dot(p.astype(BF16), v_ref[:, sl], preferred_element_type=F32)
        o_ref[:, sl] = (o * z_ref[:, sl].astype(F32)).astype(o_ref.dtype)


def _xattn_branch(q, z_gate, kv, mem_len, tm, seq):
    n = q.shape[0]
    tiles_per_seq = seq // tm
    return pl.pallas_call(
        _xattn_kernel,
        out_shape=jax.ShapeDtypeStruct((n, D_MODEL), BF16),
        grid=(n // tm,),
        in_specs=[pl.BlockSpec((tm, D_MODEL), lambda i: (i, 0)),
                  pl.BlockSpec((mem_len, D_MODEL), lambda i: (i // tiles_per_seq, 0)),
                  pl.BlockSpec((mem_len, D_MODEL), lambda i: (i // tiles_per_seq, 1)),
                  pl.BlockSpec((tm, D_MODEL), lambda i: (i, 0))],
        out_specs=pl.BlockSpec((tm, D_MODEL), lambda i: (i, 0)),
        compiler_params=_params("parallel"),
        name="xattn",
    )(q, kv, kv, z_gate)


def _merge_kernel(ac_ref, am_ref, ax_ref, wc_ref, wm_ref, wx_ref, gc_ref, gm_ref, gx_ref, o_ref):
    acc = gc_ref[...].astype(F32) * jnp.dot(ac_ref[...], wc_ref[...], preferred_element_type=F32)
    acc = acc + gm_ref[...].astype(F32) * jnp.dot(am_ref[...], wm_ref[...], preferred_element_type=F32)
    acc = acc + gx_ref[...].astype(F32) * jnp.dot(ax_ref[...], wx_ref[...], preferred_element_type=F32)
    o_ref[...] = acc.astype(o_ref.dtype)


def _merge(a_c, a_m, a_x, w_c, w_m, w_x, gates, gate_col, tm, tn):
    n, d = a_c.shape
    ncol = D_MODEL // tn
    act = pl.BlockSpec((tm, d), lambda j, i: (i, 0))
    wsp = pl.BlockSpec((d, tn), lambda j, i: (0, j))

    def gate(k):
        return pl.BlockSpec((tm, tn), lambda j, i: (i, (gate_col + k) * ncol + j))

    return pl.pallas_call(
        _merge_kernel,
        out_shape=jax.ShapeDtypeStruct((n, D_MODEL), BF16),
        grid=(ncol, n // tm),
        in_specs=[act, act, act, wsp, wsp, wsp, gate(0), gate(1), gate(2)],
        out_specs=pl.BlockSpec((tm, tn), lambda j, i: (i, j)),
        compiler_params=_params("parallel", "parallel"),
        name="merge",
    )(a_c, a_m, a_x, w_c, w_m, w_x, gates, gates, gates)


def _final_kernel(m_ref, w_ref, g_ref, x_ref, o_ref):
    y = jnp.dot(m_ref[...], w_ref[...], preferred_element_type=F32)
    y = y * lax.rsqrt(jnp.mean(y * y, axis=-1, keepdims=True) + EPS) * g_ref[...]
    o_ref[...] = x_ref[...] + y


def _final(merged, w_out, g_post, x2, tm):
    n, d = x2.shape
    return pl.pallas_call(
        _final_kernel,
        out_shape=jax.ShapeDtypeStruct((n, d), F32),
        grid=(n // tm,),
        in_specs=[pl.BlockSpec((tm, d), lambda i: (i, 0)),
                  pl.BlockSpec((d, d), lambda i: (0, 0)),
                  pl.BlockSpec((1, d), lambda i: (0, 0)),
                  pl.BlockSpec((tm, d), lambda i: (i, 0))],
        out_specs=pl.BlockSpec((tm, d), lambda i: (i, 0)),
        compiler_params=_params("parallel"),
        name="final",
    )(merged, w_out, g_post, x2)


def kernel(x, mem, g_pre, w_in, b_if, w_qk_conv, w_dw, b_dw, g_ln, b_ln, w_conv_out, g_ml_head,
           w_ml_out, g_mem, w_mem_kv, w_xa_out, w_out, g_post):
    batch, seq, d = x.shape
    mem_len = mem.shape[1]
    n = batch * seq
    assert d == D_MODEL and seq % ML_CHUNK == 0

    nif = 2 * N_HEADS
    if0 = 8 * D_MODEL
    w_in_t = w_in.T
    w_head = w_in_t.astype(BF16)
    w_tail = w_head[if0 + nif:]

    w_if = jnp.pad(w_in[:, 8 * D_MODEL:8 * D_MODEL + nif], ((0, 0), (0, LANES - nif)))
    w_if_hi = w_if.astype(BF16)
    w_if_lo = (w_if - w_if_hi.astype(F32)).astype(BF16)
    w_if_hl = jnp.concatenate([w_if_hi, w_if_lo], axis=1)
    b_if_pad = jnp.pad(b_if, (0, LANES - nif)).reshape(1, LANES)

    def row(vec):
        return vec.reshape(1, -1).astype(F32)

    def lane_cols(vec):
        return vec.reshape(d // LANES, 1, LANES).astype(F32)

    x2 = x.reshape(n, d)
    h, gif, gif_t = _prenorm(x2, row(g_pre), w_if_hl, b_if_pad, tm=1024, batch=batch)

    def proj(w_t, group, ngroups, act):
        return _proj(h, w_t, group * D_MODEL, ngroups * D_MODEL, act, tm=1024, tn=2048)

    u = _glu(h, w_head, tm=1024, tn=1024)
    z_conv = proj(w_head, 2, 1, _silu_t)
    qkv = proj(w_head, 3, 3, _identity)
    o_gate = proj(w_head, 6, 1, _sigmoid_t)
    z_ml = proj(w_head, 7, 1, _silu_t)
    q_xa = proj(w_tail, 0, 1, _identity)
    z_xa = proj(w_tail, 1, 1, _silu_t)
    gates = proj(w_tail, 2, 3, _sigmoid_t)

    w_dw_b = jnp.broadcast_to(w_dw.astype(F32)[:, None, :], (CONV_WIDTH, SUBLANES, d))
    a_c = _conv_branch(u, z_conv, 0, w_dw_b, lane_cols(b_dw), lane_cols(g_ln),
                       lane_cols(b_ln), tm=512, seq=seq)

    cw = w_qk_conv.astype(F32).reshape(QK_CONV_WIDTH, 2, d).transpose(1, 0, 2)
    t_idx = jnp.arange(ML_CHUNK)
    shifts = jnp.stack([(t_idx[:, None] - s == t_idx[None, :]).astype(BF16)
                        for s in range(1, QK_CONV_WIDTH)])
    a_m = _mlstm_branch(qkv, o_gate, z_ml, cw, shifts, gif, gif_t, row(g_ml_head), batch, seq)

    mem_h = _memnorm(mem.reshape(batch * mem_len, d), row(g_mem), tm=256)
    kv = _proj(mem_h, w_mem_kv.astype(BF16), 0, 2 * D_MODEL, _identity, tm=256, tn=2048,
               out_major=False)
    a_x = _xattn_branch(q_xa, z_xa, kv, mem_len, tm=1024, seq=seq)

    merged = _merge(a_c, a_m, a_x, w_conv_out.astype(BF16), w_ml_out.astype(BF16),
                    w_xa_out.astype(BF16), gates, 0, tm=512, tn=1024)
    out = _final(merged, w_out.astype(BF16), row(g_post), x2, tm=512)
    return out.reshape(batch, seq, d)
```

```python
import functools

import jax
import jax.numpy as jnp
from jax import lax
from jax.experimental import pallas as pl
from jax.experimental.pallas import tpu as pltpu

F32 = jnp.float32
BF16 = jnp.bfloat16

D_MODEL = 2048
N_HEADS = 4
HEAD_DIM = D_MODEL // N_HEADS
CONV_WIDTH = 31
QK_CONV_WIDTH = 4
EPS = 1e-6
NEG_BIG = -1e30

LANES = 128
SUBLANES = 8
VMEM_LIMIT = 56 * 1024 * 1024


class Tiles:
    prenorm_tm = 1024
    proj_tm, proj_tn = 1024, 2048
    glu_tm, glu_tn = 1024, 1024
    conv_tm = 1024
    xattn_tm = 1024
    mem_tm = 256
    merge_tm, merge_tn = 512, 1024
    final_tm = 512


def _sigmoid_t(y):
    return 0.5 * jnp.tanh(0.5 * y) + 0.5


def _silu_t(y):
    t = 0.5 * y
    return t * (jnp.tanh(t) + 1.0)


def _identity(y):
    return y


def _params(*sem):
    return pltpu.CompilerParams(dimension_semantics=sem, vmem_limit_bytes=VMEM_LIMIT)


def _prenorm_kernel(x_ref, g_ref, whl_ref, bif_ref, h_ref, gif_ref, gift_ref):
    xf = x_ref[...]
    y = xf * lax.rsqrt(jnp.mean(xf * xf, axis=-1, keepdims=True) + EPS) * g_ref[...]
    hi = y.astype(BF16)
    h_ref[...] = hi
    lo = (y - hi.astype(F32)).astype(BF16)
    hh = jnp.dot(hi, whl_ref[...], preferred_element_type=F32)
    lh = jnp.dot(lo, whl_ref[:, 0:LANES], preferred_element_type=F32)
    gif = hh[:, 0:LANES] + (hh[:, LANES:] + lh) + bif_ref[...]
    gif_ref[...] = gif
    gift_ref[0] = gif.T[0:SUBLANES, :]


def _prenorm(x2, g, w_if_hl, bif, tm, batch):
    n, d = x2.shape
    tiles_per_seq = n // batch // tm
    return pl.pallas_call(
        _prenorm_kernel,
        out_shape=(jax.ShapeDtypeStruct((n, d), BF16), jax.ShapeDtypeStruct((n, LANES), F32),
                   jax.ShapeDtypeStruct((batch, SUBLANES, n // batch), F32)),
        grid=(n // tm,),
        in_specs=[pl.BlockSpec((tm, d), lambda i: (i, 0)),
                  pl.BlockSpec((1, d), lambda i: (0, 0)),
                  pl.BlockSpec((d, 2 * LANES), lambda i: (0, 0)),
                  pl.BlockSpec((1, LANES), lambda i: (0, 0))],
        out_specs=(pl.BlockSpec((tm, d), lambda i: (i, 0)),
                   pl.BlockSpec((tm, LANES), lambda i: (i, 0)),
                   pl.BlockSpec((1, SUBLANES, tm),
                                lambda i: (i // tiles_per_seq, 0, i % tiles_per_seq))),
        compiler_params=_params("parallel"),
        name="prenorm",
    )(x2, g, w_if_hl, bif)


def _memnorm_kernel(x_ref, g_ref, h_ref):
    xf = x_ref[...]
    y = xf * lax.rsqrt(jnp.mean(xf * xf, axis=-1, keepdims=True) + EPS) * g_ref[...]
    h_ref[...] = y.astype(BF16)


def _memnorm(x2, g, tm):
    n, d = x2.shape
    return pl.pallas_call(
        _memnorm_kernel,
        out_shape=jax.ShapeDtypeStruct((n, d), BF16),
        grid=(n // tm,),
        in_specs=[pl.BlockSpec((tm, d), lambda i: (i, 0)),
                  pl.BlockSpec((1, d), lambda i: (0, 0))],
        out_specs=pl.BlockSpec((tm, d), lambda i: (i, 0)),
        compiler_params=_params("parallel"),
        name="memnorm",
    )(x2, g)


PROJ_SUB = 512


def _dot_t(a, w_t):
    return lax.dot_general(a, w_t, (((1,), (1,)), ((), ())), preferred_element_type=F32)


def _proj_kernel(h_ref, w_ref, o_ref, *, act, out_major):
    hh = h_ref[...]
    for c0 in range(0, o_ref.shape[1], PROJ_SUB):
        sl = slice(c0, c0 + PROJ_SUB)
        y = (_dot_t(hh, w_ref[sl, :]) if out_major
             else jnp.dot(hh, w_ref[:, sl], preferred_element_type=F32))
        o_ref[:, sl] = act(y).astype(o_ref.dtype)


def _proj(h, w, start, ncols, act, tm, tn, out_major=True):
    n, d = h.shape
    off = start // tn
    wspec = (pl.BlockSpec((tn, d), lambda j, i: (j + off, 0)) if out_major
             else pl.BlockSpec((d, tn), lambda j, i: (0, j + off)))
    return pl.pallas_call(
        functools.partial(_proj_kernel, act=act, out_major=out_major),
        out_shape=jax.ShapeDtypeStruct((n, ncols), BF16),
        grid=(ncols // tn, n // tm),
        in_specs=[pl.BlockSpec((tm, d), lambda j, i: (i, 0)), wspec],
        out_specs=pl.BlockSpec((tm, tn), lambda j, i: (i, j)),
        compiler_params=_params("parallel", "parallel"),
        name="proj",
    )(h, w)


def _glu_kernel(h_ref, wa_ref, wb_ref, o_ref):
    hh = h_ref[...]
    a = _dot_t(hh, wa_ref[...])
    b = _dot_t(hh, wb_ref[...])
    o_ref[...] = (a * _sigmoid_t(b)).astype(o_ref.dtype)


def _glu(h, w_t, tm, tn):
    n, d = h.shape
    ncol = D_MODEL // tn
    return pl.pallas_call(
        _glu_kernel,
        out_shape=jax.ShapeDtypeStruct((n, D_MODEL), BF16),
        grid=(ncol, n // tm),
        in_specs=[pl.BlockSpec((tm, d), lambda j, i: (i, 0)),
                  pl.BlockSpec((tn, d), lambda j, i: (j, 0)),
                  pl.BlockSpec((tn, d), lambda j, i: (j + ncol, 0))],
        out_specs=pl.BlockSpec((tm, tn), lambda j, i: (i, j)),
        compiler_params=_params("parallel", "parallel"),
        name="proj_glu",
    )(h, w_t, w_t)


CONV_HALO = 32
CONV_KB = 8
CONV_RB = 16


def _conv_kernel(u_ref, prev_ref, z_ref, w_ref, bdw_ref, gln_ref, bln_ref, o_ref,
                 slab_ref, acc_ref, *, tm, tiles_per_seq):
    i = pl.program_id(0)
    first = (i % tiles_per_seq) == 0
    seg = tm // SUBLANES
    pitch = seg + 1
    base = CONV_HALO - (CONV_WIDTH - 1)
    assert CONV_HALO < seg and seg % CONV_HALO == 0

    def gap(nrow):
        return nrow + nrow // seg

    def lane_col(c, carry):
        l0 = pl.multiple_of(c * LANES, LANES)
        lanes = pl.ds(l0, LANES)
        slab_ref[0:CONV_HALO, :] = jnp.where(first, 0.0, prev_ref[:, lanes].astype(F32))
        slab_ref[CONV_HALO:seg, :] = u_ref[0:seg - CONV_HALO, lanes].astype(F32)
        for q in range(1, SUBLANES):
            slab_ref[q * pitch:q * pitch + seg, :] = (
                u_ref[q * seg - CONV_HALO:(q + 1) * seg - CONV_HALO, lanes].astype(F32))
        slab_ref[SUBLANES * pitch:SUBLANES * pitch + CONV_HALO, :] = (
            u_ref[tm - CONV_HALO:tm, lanes].astype(F32))
        for k0 in range(0, seg, CONV_KB):
            data = [slab_ref[pl.ds(gap(k0 + base + m), SUBLANES, stride=pitch), :]
                    for m in range(CONV_KB + CONV_WIDTH - 1)]
            accs = [jnp.zeros((SUBLANES, LANES), F32) for _ in range(CONV_KB)]
            for j in range(CONV_WIDTH):
                wj = w_ref[j, :, lanes]
                for kk in range(CONV_KB):
                    accs[kk] = accs[kk] + wj * data[kk + j]
            for kk in range(CONV_KB):
                acc_ref[c, pl.ds(k0 + kk, SUBLANES, stride=seg), :] = accs[kk]
        return carry

    lax.fori_loop(0, D_MODEL // LANES, lane_col, 0)

    def row_block(rb, carry):
        r0 = pl.multiple_of(rb * CONV_RB, CONV_RB)
        rows = pl.ds(r0, CONV_RB)
        y = acc_ref[:, rows, :] + bdw_ref[...]
        mu = jnp.sum(jnp.sum(y, axis=0), axis=-1, keepdims=True) * (1.0 / D_MODEL)
        yc = y - mu
        var = jnp.sum(jnp.sum(yc * yc, axis=0), axis=-1, keepdims=True) * (1.0 / D_MODEL)
        yn = yc * lax.rsqrt(var + EPS) * gln_ref[...] + bln_ref[...]
        for c in range(D_MODEL // LANES):
            sl = slice(c * LANES, (c + 1) * LANES)
            o_ref[rows, sl] = (_silu_t(yn[c]) * z_ref[rows, sl].astype(F32)).astype(o_ref.dtype)
        return carry

    lax.fori_loop(0, tm // CONV_RB, row_block, 0, unroll=8)


def _conv_branch(u, z, z_col, w_dw_b, b_dw, g_ln, b_ln, tm, seq):
    n, d = u.shape
    tiles_per_seq = seq // tm
    hb = tm // CONV_HALO
    vec = pl.BlockSpec((d // LANES, 1, LANES), lambda i: (0, 0, 0))
    return pl.pallas_call(
        functools.partial(_conv_kernel, tm=tm, tiles_per_seq=tiles_per_seq),
        out_shape=jax.ShapeDtypeStruct((n, d), BF16),
        grid=(n // tm,),
        in_specs=[pl.BlockSpec((tm, d), lambda i: (i, 0)),
                  pl.BlockSpec((CONV_HALO, d), lambda i: (jnp.maximum(i * hb - 1, 0), 0)),
                  pl.BlockSpec((tm, d), lambda i: (i, z_col)),
                  pl.BlockSpec((CONV_WIDTH, SUBLANES, d), lambda i: (0, 0, 0)),
                  vec, vec, vec],
        out_specs=pl.BlockSpec((tm, d), lambda i: (i, 0)),
        scratch_shapes=[pltpu.VMEM((tm + CONV_HALO + SUBLANES, LANES), F32),
                        pltpu.VMEM((d // LANES, tm, LANES), F32)],
        compiler_params=_params("parallel"),
        name="conv_branch",
    )(u, u, z, w_dw_b, b_dw, g_ln, b_ln)


ML_CHUNK = 256
QK_HALO = 16


def _log_sigmoid(t):
    return jnp.minimum(t, 0.0) - jnp.log(1.0 + jnp.exp(-jnp.abs(t)))


def _mlstm_kernel(q_ref, k_ref, qprev_ref, kprev_ref, v_ref, cw_ref, shift_ref, gc_ref, gr_ref,
                  o_ref, z_ref, gh_ref, out_ref, c_ref, n_ref, m_ref):
    first = pl.program_id(1) == 0
    L = ML_CHUNK

    @pl.when(first)
    def _():
        c_ref[...] = jnp.zeros_like(c_ref)
        n_ref[...] = jnp.zeros_like(n_ref)
        m_ref[...] = jnp.zeros_like(m_ref)

    def conv4(pre_ref, prev_ref, a):
        pre = pre_ref[...]
        w = cw_ref[a]
        acc = w[QK_CONV_WIDTH - 1:QK_CONV_WIDTH, :] * pre.astype(F32)
        for s in range(1, QK_CONV_WIDTH):
            sh = jnp.dot(shift_ref[s - 1], pre, preferred_element_type=F32)
            acc = acc + w[QK_CONV_WIDTH - 1 - s:QK_CONV_WIDTH - s, :] * sh
        tail = jnp.where(first, 0.0, prev_ref[QK_HALO - SUBLANES:, :].astype(F32))
        rid = lax.broadcasted_iota(jnp.int32, tail.shape, 0)
        fix = jnp.zeros_like(tail)
        for s in range(1, QK_CONV_WIDTH):
            rolled = pltpu.roll(tail, s, axis=0)
            fix = fix + w[QK_CONV_WIDTH - 1 - s:QK_CONV_WIDTH - s, :] * jnp.where(rid < s, rolled, 0.0)
        acc = jnp.concatenate([acc[0:SUBLANES, :] + fix, acc[SUBLANES:, :]], axis=0)
        return _silu_t(acc)

    q_all = conv4(q_ref, qprev_ref, 0)
    k_all = conv4(k_ref, kprev_ref, 1) * (HEAD_DIM ** -0.5)

    ri = lax.broadcasted_iota(jnp.int32, (L, L), 0)
    ci = lax.broadcasted_iota(jnp.int32, (L, L), 1)
    causal = ri >= ci
    upper = ri <= ci
    gcol = gc_ref[...]
    lane = lax.broadcasted_iota(jnp.int32, gcol.shape, 1)
    grow = gr_ref[0]
    sub = lax.broadcasted_iota(jnp.int32, grow.shape, 0)

    for h in range(N_HEADS):
        hs = slice(h * HEAD_DIM, (h + 1) * HEAD_DIM)
        q = q_all[:, hs]
        k = k_all[:, hs]
        v = v_ref[:, hs]
        qb = q.astype(BF16)

        li_col = jnp.sum(jnp.where(lane == h, gcol, 0.0), axis=-1, keepdims=True)
        f_col = jnp.sum(jnp.where(lane == h + N_HEADS, gcol, 0.0), axis=-1, keepdims=True)
        li_row = jnp.sum(jnp.where(sub == h, grow, 0.0), axis=0, keepdims=True)
        f_row = jnp.sum(jnp.where(sub == h + N_HEADS, grow, 0.0), axis=0, keepdims=True)
        lf_col = _log_sigmoid(f_col)
        lf_row = _log_sigmoid(f_row)
        b_col = jnp.sum(jnp.where(causal, lf_row, 0.0), axis=-1, keepdims=True)
        b_row = jnp.sum(jnp.where(upper, lf_col, 0.0), axis=0, keepdims=True)

        m_prev = m_ref[h, 0:1, 0:1]
        d = jnp.where(causal, b_col - b_row + li_row, NEG_BIG)
        inter = b_col + m_prev
        m_row = jnp.maximum(inter, jnp.max(d, axis=-1, keepdims=True))
        w_intra = jnp.exp(d - m_row)
        w_inter = jnp.exp(inter - m_row)

        s = lax.dot_general(qb, k.astype(BF16), (((1,), (1,)), ((), ())),
                            preferred_element_type=F32) * w_intra
        num = (jnp.dot(s.astype(BF16), v, preferred_element_type=F32)
               + w_inter * jnp.dot(qb, c_ref[h].astype(BF16), preferred_element_type=F32))
        den = (jnp.sum(s, axis=-1, keepdims=True)
               + w_inter * jnp.sum(q * n_ref[h], axis=-1, keepdims=True))
        hval = num / jnp.maximum(jnp.abs(den), jnp.exp(-m_row))

        b_last = b_col[L - 1:L, :]
        g_row = b_last - b_row + li_row
        g_col = b_last - b_col + li_col
        m_new = jnp.maximum(b_last + m_prev, jnp.max(g_row, axis=-1, keepdims=True))
        decay = jnp.exp(b_last + m_prev - m_new)
        kw = k * jnp.exp(g_col - m_new)
        c_ref[h] = decay * c_ref[h] + lax.dot_general(
            kw.astype(BF16), v, (((0,), (0,)), ((), ())), preferred_element_type=F32)
        n_ref[h] = decay * n_ref[h] + jnp.sum(kw, axis=0, keepdims=True)
        m_ref[h] = jnp.broadcast_to(m_new, m_ref.shape[1:])

        hm = o_ref[:, hs].astype(F32) * hval
        hm = hm * lax.rsqrt(jnp.mean(hm * hm, axis=-1, keepdims=True) + EPS) * gh_ref[:, hs]
        out_ref[:, hs] = (hm * z_ref[:, hs].astype(F32)).astype(out_ref.dtype)


def _mlstm_branch(qkv, o_gate, z_gate, cw, shifts, gif, gif_t, g_head, batch, seq):
    n = qkv.shape[0]
    L = ML_CHUNK
    nc = seq // L
    hb = L // QK_HALO

    def tok(col_group):
        return pl.BlockSpec((L, D_MODEL), lambda b, c: (b * nc + c, col_group))

    def prev(col_group):
        return pl.BlockSpec((QK_HALO, D_MODEL),
                            lambda b, c: (jnp.maximum((b * nc + c) * hb - 1, 0), col_group))

    return pl.pallas_call(
        _mlstm_kernel,
        out_shape=jax.ShapeDtypeStruct((n, D_MODEL), BF16),
        grid=(batch, nc),
        in_specs=[tok(0), tok(1), prev(0), prev(1), tok(2),
                  pl.BlockSpec((2, QK_CONV_WIDTH, D_MODEL), lambda b, c: (0, 0, 0)),
                  pl.BlockSpec((QK_CONV_WIDTH - 1, L, L), lambda b, c: (0, 0, 0)),
                  pl.BlockSpec((L, LANES), lambda b, c: (b * nc + c, 0)),
                  pl.BlockSpec((1, SUBLANES, L), lambda b, c: (b, 0, c)),
                  tok(0), tok(0),
                  pl.BlockSpec((1, D_MODEL), lambda b, c: (0, 0))],
        out_specs=pl.BlockSpec((L, D_MODEL), lambda b, c: (b * nc + c, 0)),
        scratch_shapes=[pltpu.VMEM((N_HEADS, HEAD_DIM, HEAD_DIM), F32),
                        pltpu.VMEM((N_HEADS, 1, HEAD_DIM), F32),
                        pltpu.VMEM((N_HEADS, SUBLANES, LANES), F32)],
        compiler_params=_params("parallel", "arbitrary"),
        name="mlstm",
    )(qkv, qkv, qkv, qkv, qkv, cw, shifts, gif, gif_t, o_gate, z_gate, g_head)


def _xattn_kernel(q_ref, k_ref, v_ref, z_ref, o_ref):
    scale = HEAD_DIM ** -0.5
    for h in range(N_HEADS):
        sl = slice(h * HEAD_DIM, (h + 1) * HEAD_DIM)
        s = lax.dot_general(q_ref[:, sl], k_ref[:, sl], (((1,), (1,)), ((), ())),
                            preferred_element_type=F32) * scale
        e = jnp.exp(s - jnp.max(s, axis=-1, keepdims=True))
        p = e / jnp.sum(e, axis=-1, keepdims=True)
        o = jnp.dot(p.astype(BF16), v_ref[:, sl], preferred_element_type=F32)
        o_ref[:, sl] = (o * z_ref[:, sl].astype(F32)).astype(o_ref.dtype)


def _xattn_branch(q, z_gate, kv, mem_len, tm, seq):
    n = q.shape[0]
    tiles_per_seq = seq // tm
    return pl.pallas_call(
        _xattn_kernel,
        out_shape=jax.ShapeDtypeStruct((n, D_MODEL), BF16),
        grid=(n // tm,),
        in_specs=[pl.BlockSpec((tm, D_MODEL), lambda i: (i, 0)),
                  pl.BlockSpec((mem_len, D_MODEL), lambda i: (i // tiles_per_seq, 0)),
                  pl.BlockSpec((mem_len, D_MODEL), lambda i: (i // tiles_per_seq, 1)),
                  pl.BlockSpec((tm, D_MODEL), lambda i: (i, 0))],
        out_specs=pl.BlockSpec((tm, D_MODEL), lambda i: (i, 0)),
        compiler_params=_params("parallel"),
        name="xattn",
    )(q, kv, kv, z_gate)


def _merge_kernel(ac_ref, am_ref, ax_ref, wc_ref, wm_ref, wx_ref, gc_ref, gm_ref, gx_ref, o_ref):
    acc = gc_ref[...].astype(F32) * jnp.dot(ac_ref[...], wc_ref[...], preferred_element_type=F32)
    acc = acc + gm_ref[...].astype(F32) * jnp.dot(am_ref[...], wm_ref[...], preferred_element_type=F32)
    acc = acc + gx_ref[...].astype(F32) * jnp.dot(ax_ref[...], wx_ref[...], preferred_element_type=F32)
    o_ref[...] = acc.astype(o_ref.dtype)


def _merge(a_c, a_m, a_x, w_c, w_m, w_x, gates, gate_col, tm, tn):
    n, d = a_c.shape
    ncol = D_MODEL // tn
    act = pl.BlockSpec((tm, d), lambda j, i: (i, 0))
    wsp = pl.BlockSpec((d, tn), lambda j, i: (0, j))

    def gate(k):
        return pl.BlockSpec((tm, tn), lambda j, i: (i, (gate_col + k) * ncol + j))

    return pl.pallas_call(
        _merge_kernel,
        out_shape=jax.ShapeDtypeStruct((n, D_MODEL), BF16),
        grid=(ncol, n // tm),
        in_specs=[act, act, act, wsp, wsp, wsp, gate(0), gate(1), gate(2)],
        out_specs=pl.BlockSpec((tm, tn), lambda j, i: (i, j)),
        compiler_params=_params("parallel", "parallel"),
        name="merge",
    )(a_c, a_m, a_x, w_c, w_m, w_x, gates, gates, gates)


def _final_kernel(m_ref, w_ref, g_ref, x_ref, o_ref):
    y = jnp.dot(m_ref[...], w_ref[...], preferred_element_type=F32)
    y = y * lax.rsqrt(jnp.mean(y * y, axis=-1, keepdims=True) + EPS) * g_ref[...]
    o_ref[...] = x_ref[...] + y


def _final(merged, w_out, g_post, x2, tm):
    n, d = x2.shape
    return pl.pallas_call(
        _final_kernel,
        out_shape=jax.ShapeDtypeStruct((n, d), F32),
        grid=(n // tm,),
        in_specs=[pl.BlockSpec((tm, d), lambda i: (i, 0)),
                  pl.BlockSpec((d, d), lambda i: (0, 0)),
                  pl.BlockSpec((1, d), lambda i: (0, 0)),
                  pl.BlockSpec((tm, d), lambda i: (i, 0))],
        out_specs=pl.BlockSpec((tm, d), lambda i: (i, 0)),
        compiler_params=_params("parallel"),
        name="final",
    )(merged, w_out, g_post, x2)


def kernel(x, mem, g_pre, w_in, b_if, w_qk_conv, w_dw, b_dw, g_ln, b_ln, w_conv_out, g_ml_head,
           w_ml_out, g_mem, w_mem_kv, w_xa_out, w_out, g_post):
    batch, seq, d = x.shape
    mem_len = mem.shape[1]
    n = batch * seq
    assert d == D_MODEL and seq % ML_CHUNK == 0
    for t in (Tiles.prenorm_tm, Tiles.conv_tm, Tiles.xattn_tm):
        assert seq % t == 0, "these tiles must not straddle two sequences"
    for t in (Tiles.proj_tm, Tiles.glu_tm, Tiles.merge_tm, Tiles.final_tm):
        assert n % t == 0

    nif = 2 * N_HEADS
    if0 = 8 * D_MODEL
    w_in_t = w_in.T
    w_head = w_in_t.astype(BF16)
    w_tail = w_head[if0 + nif:]

    w_if = jnp.pad(w_in[:, 8 * D_MODEL:8 * D_MODEL + nif], ((0, 0), (0, LANES - nif)))
    w_if_hi = w_if.astype(BF16)
    w_if_lo = (w_if - w_if_hi.astype(F32)).astype(BF16)
    w_if_hl = jnp.concatenate([w_if_hi, w_if_lo], axis=1)
    b_if_pad = jnp.pad(b_if, (0, LANES - nif)).reshape(1, LANES)

    def row(vec):
        return vec.reshape(1, -1).astype(F32)

    def lane_cols(vec):
        return vec.reshape(d // LANES, 1, LANES).astype(F32)

    x2 = x.reshape(n, d)
    h, gif, gif_t = _prenorm(x2, row(g_pre), w_if_hl, b_if_pad, tm=Tiles.prenorm_tm, batch=batch)

    def proj(w_t, group, ngroups, act):
        return _proj(h, w_t, group * D_MODEL, ngroups * D_MODEL, act,
                     tm=Tiles.proj_tm, tn=Tiles.proj_tn)

    u = _glu(h, w_head, tm=Tiles.glu_tm, tn=Tiles.glu_tn)
    z_conv = proj(w_head, 2, 1, _silu_t)
    qkv = proj(w_head, 3, 3, _identity)
    o_gate = proj(w_head, 6, 1, _sigmoid_t)
    z_ml = proj(w_head, 7, 1, _silu_t)
    q_xa = proj(w_tail, 0, 1, _identity)
    z_xa = proj(w_tail, 1, 1, _silu_t)
    gates = proj(w_tail, 2, 3, _sigmoid_t)

    w_dw_b = jnp.broadcast_to(w_dw.astype(F32)[:, None, :], (CONV_WIDTH, SUBLANES, d))
    a_c = _conv_branch(u, z_conv, 0, w_dw_b, lane_cols(b_dw), lane_cols(g_ln),
                       lane_cols(b_ln), tm=Tiles.conv_tm, seq=seq)

    cw = w_qk_conv.astype(F32).reshape(QK_CONV_WIDTH, 2, d).transpose(1, 0, 2)
    t_idx = jnp.arange(ML_CHUNK)
    shifts = jnp.stack([(t_idx[:, None] - s == t_idx[None, :]).astype(BF16)
                        for s in range(1, QK_CONV_WIDTH)])
    a_m = _mlstm_branch(qkv, o_gate, z_ml, cw, shifts, gif, gif_t, row(g_ml_head), batch, seq)

    mem_h = _memnorm(mem.reshape(batch * mem_len, d), row(g_mem), tm=Tiles.mem_tm)
    kv = _proj(mem_h, w_mem_kv.astype(BF16), 0, 2 * D_MODEL, _identity, tm=Tiles.mem_tm,
               tn=Tiles.proj_tn, out_major=False)
    a_x = _xattn_branch(q_xa, z_xa, kv, mem_len, tm=Tiles.xattn_tm, seq=seq)

    merged = _merge(a_c, a_m, a_x, w_conv_out.astype(BF16), w_ml_out.astype(BF16),
                    w_xa_out.astype(BF16), gates, 0, tm=Tiles.merge_tm, tn=Tiles.merge_tn)
    out = _final(merged, w_out.astype(BF16), row(g_post), x2, tm=Tiles.final_tm)
    return out.reshape(batch, seq, d)
```

```python
import functools

import jax
import jax.numpy as jnp
from jax import lax
from jax.experimental import pallas as pl
from jax.experimental.pallas import tpu as pltpu

F32 = jnp.float32
BF16 = jnp.bfloat16

D_MODEL = 2048
N_HEADS = 4
HEAD_DIM = D_MODEL // N_HEADS
CONV_WIDTH = 31
QK_CONV_WIDTH = 4
EPS = 1e-6
NEG_BIG = -1e30

LANES = 128
SUBLANES = 8
VMEM_LIMIT = 56 * 1024 * 1024


class Tiles:
    prenorm_tm = 1024
    proj_tm, proj_tn = 1024, 2048
    glu_tm, glu_tn = 1024, 1024
    conv_tm = 1024
    xattn_tm = 1024
    mem_tm = 256
    merge_tm, merge_tn = 512, 1024
    final_tm = 512


def _sigmoid_t(y):
    return 0.5 * jnp.tanh(0.5 * y) + 0.5


def _silu_t(y):
    t = 0.5 * y
    return t * (jnp.tanh(t) + 1.0)


def _identity(y):
    return y


def _params(*sem):
    return pltpu.CompilerParams(dimension_semantics=sem, vmem_limit_bytes=VMEM_LIMIT)


def _prenorm_kernel(x_ref, g_ref, whl_ref, bif_ref, h_ref, gif_ref, gift_ref):
    xf = x_ref[...]
    y = xf * lax.rsqrt(jnp.mean(xf * xf, axis=-1, keepdims=True) + EPS) * g_ref[...]
    hi = y.astype(BF16)
    h_ref[...] = hi
    lo = (y - hi.astype(F32)).astype(BF16)
    hh = jnp.dot(hi, whl_ref[...], preferred_element_type=F32)
    lh = jnp.dot(lo, whl_ref[:, 0:LANES], preferred_element_type=F32)
    gif = hh[:, 0:LANES] + (hh[:, LANES:] + lh) + bif_ref[...]
    gif_ref[...] = gif
    gift_ref[0] = gif.T[0:SUBLANES, :]


def _prenorm(x2, g, w_if_hl, bif, tm, batch):
    n, d = x2.shape
    tiles_per_seq = n // batch // tm
    return pl.pallas_call(
        _prenorm_kernel,
        out_shape=(jax.ShapeDtypeStruct((n, d), BF16), jax.ShapeDtypeStruct((n, LANES), F32),
                   jax.ShapeDtypeStruct((batch, SUBLANES, n // batch), F32)),
        grid=(n // tm,),
        in_specs=[pl.BlockSpec((tm, d), lambda i: (i, 0)),
                  pl.BlockSpec((1, d), lambda i: (0, 0)),
                  pl.BlockSpec((d, 2 * LANES), lambda i: (0, 0)),
                  pl.BlockSpec((1, LANES), lambda i: (0, 0))],
        out_specs=(pl.BlockSpec((tm, d), lambda i: (i, 0)),
                   pl.BlockSpec((tm, LANES), lambda i: (i, 0)),
                   pl.BlockSpec((1, SUBLANES, tm),
                                lambda i: (i // tiles_per_seq, 0, i % tiles_per_seq))),
        compiler_params=_params("parallel"),
        name="prenorm",
    )(x2, g, w_if_hl, bif)


def _memnorm_kernel(x_ref, g_ref, h_ref):
    xf = x_ref[...]
    y = xf * lax.rsqrt(jnp.mean(xf * xf, axis=-1, keepdims=True) + EPS) * g_ref[...]
    h_ref[...] = y.astype(BF16)


def _memnorm(x2, g, tm):
    n, d = x2.shape
    return pl.pallas_call(
        _memnorm_kernel,
        out_shape=jax.ShapeDtypeStruct((n, d), BF16),
        grid=(n // tm,),
        in_specs=[pl.BlockSpec((tm, d), lambda i: (i, 0)),
                  pl.BlockSpec((1, d), lambda i: (0, 0))],
        out_specs=pl.BlockSpec((tm, d), lambda i: (i, 0)),
        compiler_params=_params("parallel"),
        name="memnorm",
    )(x2, g)


PROJ_SUB = 512


def _dot_t(a, w_t):
    return lax.dot_general(a, w_t, (((1,), (1,)), ((), ())), preferred_element_type=F32)


def _proj_kernel(h_ref, w_ref, o_ref, *, act, out_major):
    hh = h_ref[...]
    for c0 in range(0, o_ref.shape[1], PROJ_SUB):
        sl = slice(c0, c0 + PROJ_SUB)
        y = (_dot_t(hh, w_ref[sl, :]) if out_major
             else jnp.dot(hh, w_ref[:, sl], preferred_element_type=F32))
        o_ref[:, sl] = act(y).astype(o_ref.dtype)


def _proj(h, w, start, ncols, act, tm, tn, out_major=True):
    n, d = h.shape
    off = start // tn
    wspec = (pl.BlockSpec((tn, d), lambda j, i: (j + off, 0)) if out_major
             else pl.BlockSpec((d, tn), lambda j, i: (0, j + off)))
    return pl.pallas_call(
        functools.partial(_proj_kernel, act=act, out_major=out_major),
        out_shape=jax.ShapeDtypeStruct((n, ncols), BF16),
        grid=(ncols // tn, n // tm),
        in_specs=[pl.BlockSpec((tm, d), lambda j, i: (i, 0)), wspec],
        out_specs=pl.BlockSpec((tm, tn), lambda j, i: (i, j)),
        compiler_params=_params("parallel", "parallel"),
        name="proj",
    )(h, w)


def _glu_kernel(h_ref, wa_ref, wb_ref, o_ref):
    hh = h_ref[...]
    a = _dot_t(hh, wa_ref[...])
    b = _dot_t(hh, wb_ref[...])
    o_ref[...] = (a * _sigmoid_t(b)).astype(o_ref.dtype)


def _glu(h, w_t, tm, tn):
    n, d = h.shape
    ncol = D_MODEL // tn
    return pl.pallas_call(
        _glu_kernel,
        out_shape=jax.ShapeDtypeStruct((n, D_MODEL), BF16),
        grid=(ncol, n // tm),
        in_specs=[pl.BlockSpec((tm, d), lambda j, i: (i, 0)),
                  pl.BlockSpec((tn, d), lambda j, i: (j, 0)),
                  pl.BlockSpec((tn, d), lambda j, i: (j + ncol, 0))],
        out_specs=pl.BlockSpec((tm, tn), lambda j, i: (i, j)),
        compiler_params=_params("parallel", "parallel"),
        name="proj_glu",
    )(h, w_t, w_t)


CONV_HALO = 32
CONV_KB = 16
CONV_RB = 16
PACK = 2 * SUBLANES


def _conv_kernel(u_ref, prev_ref, z_ref, w_ref, bdw_ref, gln_ref, bln_ref, o_ref,
                 slab_ref, acc_ref, *, tm, tiles_per_seq):
    i = pl.program_id(0)
    first = (i % tiles_per_seq) == 0
    seg = tm // SUBLANES
    pitch = seg + 1
    base = CONV_HALO - (CONV_WIDTH - 1)
    half = CONV_KB // 2
    ncols = D_MODEL // LANES
    assert CONV_HALO < seg and seg % CONV_HALO == 0 and seg % CONV_KB == 0 and CONV_KB == CONV_RB

    def gap(nrow):
        return nrow + nrow // seg

    def build_col(c, carry):
        lanes = pl.ds(pl.multiple_of(c * LANES, LANES), LANES)
        slab_ref[c, 0:CONV_HALO, :] = jnp.where(first, 0.0, prev_ref[:, lanes].astype(F32))
        slab_ref[c, CONV_HALO:seg, :] = u_ref[0:seg - CONV_HALO, lanes].astype(F32)
        for q in range(1, SUBLANES):
            slab_ref[c, q * pitch:q * pitch + seg, :] = (
                u_ref[q * seg - CONV_HALO:(q + 1) * seg - CONV_HALO, lanes].astype(F32))
        slab_ref[c, SUBLANES * pitch:SUBLANES * pitch + CONV_HALO, :] = (
            u_ref[tm - CONV_HALO:tm, lanes].astype(F32))
        return carry

    lax.fori_loop(0, ncols, build_col, 0)

    def mac_col(c, carry):
        for k0 in range(0, seg, CONV_KB):
            pk = [jnp.concatenate(
                      [slab_ref[c, pl.ds(gap(k0 + base + m), SUBLANES, stride=pitch), :],
                       slab_ref[c, pl.ds(gap(k0 + base + m + half), SUBLANES, stride=pitch), :]],
                      axis=0).astype(BF16)
                  for m in range(half + CONV_WIDTH - 1)]
            for kk in range(half):
                acc = None
                for j in range(CONV_WIDTH):
                    t = w_ref[j, c].astype(F32) * pk[kk + j].astype(F32)
                    acc = t if acc is None else acc + t
                blk = (k0 // CONV_KB) * half + kk
                acc_ref[c, blk * PACK:(blk + 1) * PACK, :] = acc
        return carry

    lax.fori_loop(0, ncols, mac_col, 0)

    blocks_per_seg = seg // CONV_RB

    def row_block(rb, carry):
        r = rb // blocks_per_seg
        k0 = (rb % blocks_per_seg) * CONV_RB
        lo = acc_ref[:, pl.ds(k0 * SUBLANES + r, SUBLANES, stride=PACK), :]
        hi = acc_ref[:, pl.ds(k0 * SUBLANES + SUBLANES + r, SUBLANES, stride=PACK), :]
        y = jnp.concatenate([lo, hi], axis=1) + bdw_ref[...]
        mu = jnp.sum(jnp.sum(y, axis=0), axis=-1, keepdims=True) * (1.0 / D_MODEL)
        yc = y - mu
        var = jnp.sum(jnp.sum(yc * yc, axis=0), axis=-1, keepdims=True) * (1.0 / D_MODEL)
        yn = yc * lax.rsqrt(var + EPS) * gln_ref[...] + bln_ref[...]
        rows = pl.ds(pl.multiple_of(rb * CONV_RB, CONV_RB), CONV_RB)
        for c in range(ncols):
            sl = slice(c * LANES, (c + 1) * LANES)
            o_ref[rows, sl] = (_silu_t(yn[c]) * z_ref[rows, sl].astype(F32)).astype(o_ref.dtype)
        return carry

    lax.fori_loop(0, tm // CONV_RB, row_block, 0, unroll=8)


def _conv_branch(u, z, z_col, w_dw_b, b_dw, g_ln, b_ln, tm, seq):
    n, d = u.shape
    tiles_per_seq = seq // tm
    hb = tm // CONV_HALO
    vec = pl.BlockSpec((d // LANES, 1, LANES), lambda i: (0, 0, 0))
    return pl.pallas_call(
        functools.partial(_conv_kernel, tm=tm, tiles_per_seq=tiles_per_seq),
        out_shape=jax.ShapeDtypeStruct((n, d), BF16),
        grid=(n // tm,),
        in_specs=[pl.BlockSpec((tm, d), lambda i: (i, 0)),
                  pl.BlockSpec((CONV_HALO, d), lambda i: (jnp.maximum(i * hb - 1, 0), 0)),
                  pl.BlockSpec((tm, d), lambda i: (i, z_col)),
                  pl.BlockSpec((CONV_WIDTH, d // LANES, PACK, LANES), lambda i: (0, 0, 0, 0)),
                  vec, vec, vec],
        out_specs=pl.BlockSpec((tm, d), lambda i: (i, 0)),
        scratch_shapes=[pltpu.VMEM((d // LANES, tm + CONV_HALO + SUBLANES, LANES), F32),
                        pltpu.VMEM((d // LANES, tm, LANES), F32)],
        compiler_params=_params("parallel"),
        name="conv_branch",
    )(u, u, z, w_dw_b, b_dw, g_ln, b_ln)


ML_CHUNK = 256
QK_HALO = 16


def _log_sigmoid(t):
    return jnp.minimum(t, 0.0) - jnp.log(1.0 + jnp.exp(-jnp.abs(t)))


def _mlstm_kernel(q_ref, k_ref, qprev_ref, kprev_ref, v_ref, cw_ref, shift_ref, gc_ref, gr_ref,
                  o_ref, z_ref, gh_ref, out_ref, c_ref, n_ref, m_ref):
    first = pl.program_id(1) == 0
    L = ML_CHUNK

    @pl.when(first)
    def _():
        c_ref[...] = jnp.zeros_like(c_ref)
        n_ref[...] = jnp.zeros_like(n_ref)
        m_ref[...] = jnp.zeros_like(m_ref)

    def conv4(pre_ref, prev_ref, a):
        pre = pre_ref[...]
        w = cw_ref[a]
        acc = w[QK_CONV_WIDTH - 1:QK_CONV_WIDTH, :] * pre.astype(F32)
        for s in range(1, QK_CONV_WIDTH):
            sh = jnp.dot(shift_ref[s - 1], pre, preferred_element_type=F32)
            acc = acc + w[QK_CONV_WIDTH - 1 - s:QK_CONV_WIDTH - s, :] * sh
        tail = jnp.where(first, 0.0, prev_ref[QK_HALO - SUBLANES:, :].astype(F32))
        rid = lax.broadcasted_iota(jnp.int32, tail.shape, 0)
        fix = jnp.zeros_like(tail)
        for s in range(1, QK_CONV_WIDTH):
            rolled = pltpu.roll(tail, s, axis=0)
            fix = fix + w[QK_CONV_WIDTH - 1 - s:QK_CONV_WIDTH - s, :] * jnp.where(rid < s, rolled, 0.0)
        acc = jnp.concatenate([acc[0:SUBLANES, :] + fix, acc[SUBLANES:, :]], axis=0)
        return _silu_t(acc)

    q_all = conv4(q_ref, qprev_ref, 0)
    k_all = conv4(k_ref, kprev_ref, 1) * (HEAD_DIM ** -0.5)

    ri = lax.broadcasted_iota(jnp.int32, (L, L), 0)
    ci = lax.broadcasted_iota(jnp.int32, (L, L), 1)
    causal = ri >= ci
    upper = ri <= ci
    gcol = gc_ref[...]
    lane = lax.broadcasted_iota(jnp.int32, gcol.shape, 1)
    grow = gr_ref[0]
    sub = lax.broadcasted_iota(jnp.int32, grow.shape, 0)

    for h in range(N_HEADS):
        hs = slice(h * HEAD_DIM, (h + 1) * HEAD_DIM)
        q = q_all[:, hs]
        k = k_all[:, hs]
        v = v_ref[:, hs]
        qb = q.astype(BF16)

        li_col = jnp.sum(jnp.where(lane == h, gcol, 0.0), axis=-1, keepdims=True)
        f_col = jnp.sum(jnp.where(lane == h + N_HEADS, gcol, 0.0), axis=-1, keepdims=True)
        li_row = jnp.sum(jnp.where(sub == h, grow, 0.0), axis=0, keepdims=True)
        f_row = jnp.sum(jnp.where(sub == h + N_HEADS, grow, 0.0), axis=0, keepdims=True)
        lf_col = _log_sigmoid(f_col)
        lf_row = _log_sigmoid(f_row)
        b_col = jnp.sum(jnp.where(causal, lf_row, 0.0), axis=-1, keepdims=True)
        b_row = jnp.sum(jnp.where(upper, lf_col, 0.0), axis=0, keepdims=True)

        m_prev = m_ref[h, 0:1, 0:1]
        d = jnp.where(causal, b_col - b_row + li_row, NEG_BIG)
        inter = b_col + m_prev
        m_row = jnp.maximum(inter, jnp.max(d, axis=-1, keepdims=True))
        w_intra = jnp.exp(d - m_row)
        w_inter = jnp.exp(inter - m_row)

        s = lax.dot_general(qb, k.astype(BF16), (((1,), (1,)), ((), ())),
                            preferred_element_type=F32) * w_intra
        num = (jnp.dot(s.astype(BF16), v, preferred_element_type=F32)
               + w_inter * jnp.dot(qb, c_ref[h].astype(BF16), preferred_element_type=F32))
        den = (jnp.sum(s, axis=-1, keepdims=True)
               + w_inter * jnp.sum(q * n_ref[h], axis=-1, keepdims=True))
        hval = num / jnp.maximum(jnp.abs(den), jnp.exp(-m_row))

        b_last = b_col[L - 1:L, :]
        g_row = b_last - b_row + li_row
        g_col = b_last - b_col + li_col
        m_new = jnp.maximum(b_last + m_prev, jnp.max(g_row, axis=-1, keepdims=True))
        decay = jnp.exp(b_last + m_prev - m_new)
        kw = k * jnp.exp(g_col - m_new)
        c_ref[h] = decay * c_ref[h] + lax.dot_general(
            kw.astype(BF16), v, (((0,), (0,)), ((), ())), preferred_element_type=F32)
        n_ref[h] = decay * n_ref[h] + jnp.sum(kw, axis=0, keepdims=True)
        m_ref[h] = jnp.broadcast_to(m_new, m_ref.shape[1:])

        hm = o_ref[:, hs].astype(F32) * hval
        hm = hm * lax.rsqrt(jnp.mean(hm * hm, axis=-1, keepdims=True) + EPS) * gh_ref[:, hs]
        out_ref[:, hs] = (hm * z_ref[:, hs].astype(F32)).astype(out_ref.dtype)


def _mlstm_branch(qkv, o_gate, z_gate, cw, shifts, gif, gif_t, g_head, batch, seq):
    n = qkv.shape[0]
    L = ML_CHUNK
    nc = seq // L
    hb = L // QK_HALO

    def tok(col_group):
        return pl.BlockSpec((L, D_MODEL), lambda b, c: (b * nc + c, col_group))

    def prev(col_group):
        return pl.BlockSpec((QK_HALO, D_MODEL),
                            lambda b, c: (jnp.maximum((b * nc + c) * hb - 1, 0), col_group))

    return pl.pallas_call(
        _mlstm_kernel,
        out_shape=jax.ShapeDtypeStruct((n, D_MODEL), BF16),
        grid=(batch, nc),
        in_specs=[tok(0), tok(1), prev(0), prev(1), tok(2),
                  pl.BlockSpec((2, QK_CONV_WIDTH, D_MODEL), lambda b, c: (0, 0, 0)),
                  pl.BlockSpec((QK_CONV_WIDTH - 1, L, L), lambda b, c: (0, 0, 0)),
                  pl.BlockSpec((L, LANES), lambda b, c: (b * nc + c, 0)),
                  pl.BlockSpec((1, SUBLANES, L), lambda b, c: (b, 0, c)),
                  tok(0), tok(0),
                  pl.BlockSpec((1, D_MODEL), lambda b, c: (0, 0))],
        out_specs=pl.BlockSpec((L, D_MODEL), lambda b, c: (b * nc + c, 0)),
        scratch_shapes=[pltpu.VMEM((N_HEADS, HEAD_DIM, HEAD_DIM), F32),
                        pltpu.VMEM((N_HEADS, 1, HEAD_DIM), F32),
                        pltpu.VMEM((N_HEADS, SUBLANES, LANES), F32)],
        compiler_params=_params("parallel", "arbitrary"),
        name="mlstm",
    )(qkv, qkv, qkv, qkv, qkv, cw, shifts, gif, gif_t, o_gate, z_gate, g_head)


def _xattn_kernel(q_ref, k_ref, v_ref, z_ref, o_ref):
    scale = HEAD_DIM ** -0.5
    for h in range(N_HEADS):
        sl = slice(h * HEAD_DIM, (h + 1) * HEAD_DIM)
        s = lax.dot_general(q_ref[:, sl], k_ref[:, sl], (((1,), (1,)), ((), ())),
                            preferred_element_type=F32) * scale
        e = jnp.exp(s - jnp.max(s, axis=-1, keepdims=True))
        p = e / jnp.sum(e, axis=-1, keepdims=True)
        o = jnp.dot(p.astype(BF16), v_ref[:, sl], preferred_element_type=F32)
        o_ref[:, sl] = (o * z_ref[:, sl].astype(F32)).astype(o_ref.dtype)


def _xattn_branch(q, z_gate, kv, mem_len, tm, seq):
    n = q.shape[0]
    tiles_per_seq = seq // tm
    return pl.pallas_call(
        _xattn_kernel,
        out_shape=jax.ShapeDtypeStruct((n, D_MODEL), BF16),
        grid=(n // tm,),
        in_specs=[pl.BlockSpec((tm, D_MODEL), lambda i: (i, 0)),
                  pl.BlockSpec((mem_len, D_MODEL), lambda i: (i // tiles_per_seq, 0)),
                  pl.BlockSpec((mem_len, D_MODEL), lambda i: (i // tiles_per_seq, 1)),
                  pl.BlockSpec((tm, D_MODEL), lambda i: (i, 0))],
        out_specs=pl.BlockSpec((tm, D_MODEL), lambda i: (i, 0)),
        compiler_params=_params("parallel"),
        name="xattn",
    )(q, kv, kv, z_gate)


def _merge_kernel(ac_ref, am_ref, ax_ref, wc_ref, wm_ref, wx_ref, gc_ref, gm_ref, gx_ref, o_ref):
    acc = gc_ref[...].astype(F32) * jnp.dot(ac_ref[...], wc_ref[...], preferred_element_type=F32)
    acc = acc + gm_ref[...].astype(F32) * jnp.dot(am_ref[...], wm_ref[...], preferred_element_type=F32)
    acc = acc + gx_ref[...].astype(F32) * jnp.dot(ax_ref[...], wx_ref[...], preferred_element_type=F32)
    o_ref[...] = acc.astype(o_ref.dtype)


def _merge(a_c, a_m, a_x, w_c, w_m, w_x, gates, gate_col, tm, tn):
    n, d = a_c.shape
    ncol = D_MODEL // tn
    act = pl.BlockSpec((tm, d), lambda j, i: (i, 0))
    wsp = pl.BlockSpec((d, tn), lambda j, i: (0, j))

    def gate(k):
        return pl.BlockSpec((tm, tn), lambda j, i: (i, (gate_col + k) * ncol + j))

    return pl.pallas_call(
        _merge_kernel,
        out_shape=jax.ShapeDtypeStruct((n, D_MODEL), BF16),
        grid=(ncol, n // tm),
        in_specs=[act, act, act, wsp, wsp, wsp, gate(0), gate(1), gate(2)],
        out_specs=pl.BlockSpec((tm, tn), lambda j, i: (i, j)),
        compiler_params=_params("parallel", "parallel"),
        name="merge",
    )(a_c, a_m, a_x, w_c, w_m, w_x, gates, gates, gates)


def _final_kernel(m_ref, w_ref, g_ref, x_ref, o_ref):
    y = jnp.dot(m_ref[...], w_ref[...], preferred_element_type=F32)
    y = y * lax.rsqrt(jnp.mean(y * y, axis=-1, keepdims=True) + EPS) * g_ref[...]
    o_ref[...] = x_ref[...] + y


def _final(merged, w_out, g_post, x2, tm):
    n, d = x2.shape
    return pl.pallas_call(
        _final_kernel,
        out_shape=jax.ShapeDtypeStruct((n, d), F32),
        grid=(n // tm,),
        in_specs=[pl.BlockSpec((tm, d), lambda i: (i, 0)),
                  pl.BlockSpec((d, d), lambda i: (0, 0)),
                  pl.BlockSpec((1, d), lambda i: (0, 0)),
                  pl.BlockSpec((tm, d), lambda i: (i, 0))],
        out_specs=pl.BlockSpec((tm, d), lambda i: (i, 0)),
        compiler_params=_params("parallel"),
        name="final",
    )(merged, w_out, g_post, x2)


def kernel(x, mem, g_pre, w_in, b_if, w_qk_conv, w_dw, b_dw, g_ln, b_ln, w_conv_out, g_ml_head,
           w_ml_out, g_mem, w_mem_kv, w_xa_out, w_out, g_post):
    batch, seq, d = x.shape
    mem_len = mem.shape[1]
    n = batch * seq
    assert d == D_MODEL and seq % ML_CHUNK == 0
    for t in (Tiles.prenorm_tm, Tiles.conv_tm, Tiles.xattn_tm):
        assert seq % t == 0, "these tiles must not straddle two sequences"
    for t in (Tiles.proj_tm, Tiles.glu_tm, Tiles.merge_tm, Tiles.final_tm):
        assert n % t == 0

    nif = 2 * N_HEADS
    if0 = 8 * D_MODEL
    w_in_t = w_in.T
    w_head = w_in_t.astype(BF16)
    w_tail = w_head[if0 + nif:]

    w_if = jnp.pad(w_in[:, 8 * D_MODEL:8 * D_MODEL + nif], ((0, 0), (0, LANES - nif)))
    w_if_hi = w_if.astype(BF16)
    w_if_lo = (w_if - w_if_hi.astype(F32)).astype(BF16)
    w_if_hl = jnp.concatenate([w_if_hi, w_if_lo], axis=1)
    b_if_pad = jnp.pad(b_if, (0, LANES - nif)).reshape(1, LANES)

    def row(vec):
        return vec.reshape(1, -1).astype(F32)

    def lane_cols(vec):
        return vec.reshape(d // LANES, 1, LANES).astype(F32)

    x2 = x.reshape(n, d)
    h, gif, gif_t = _prenorm(x2, row(g_pre), w_if_hl, b_if_pad, tm=Tiles.prenorm_tm, batch=batch)

    def proj(w_t, group, ngroups, act):
        return _proj(h, w_t, group * D_MODEL, ngroups * D_MODEL, act,
                     tm=Tiles.proj_tm, tn=Tiles.proj_tn)

    u = _glu(h, w_head, tm=Tiles.glu_tm, tn=Tiles.glu_tn)
    z_conv = proj(w_head, 2, 1, _silu_t)
    qkv = proj(w_head, 3, 3, _identity)
    o_gate = proj(w_head, 6, 1, _sigmoid_t)
    z_ml = proj(w_head, 7, 1, _silu_t)
    q_xa = proj(w_tail, 0, 1, _identity)
    z_xa = proj(w_tail, 1, 1, _silu_t)
    gates = proj(w_tail, 2, 3, _sigmoid_t)

    w_dw_b = jnp.broadcast_to(w_dw.astype(BF16).reshape(CONV_WIDTH, d // LANES, 1, LANES),
                              (CONV_WIDTH, d // LANES, PACK, LANES))
    a_c = _conv_branch(u, z_conv, 0, w_dw_b, lane_cols(b_dw), lane_cols(g_ln),
                       lane_cols(b_ln), tm=Tiles.conv_tm, seq=seq)

    cw = w_qk_conv.astype(F32).reshape(QK_CONV_WIDTH, 2, d).transpose(1, 0, 2)
    t_idx = jnp.arange(ML_CHUNK)
    shifts = jnp.stack([(t_idx[:, None] - s == t_idx[None, :]).astype(BF16)
                        for s in range(1, QK_CONV_WIDTH)])
    a_m = _mlstm_branch(qkv, o_gate, z_ml, cw, shifts, gif, gif_t, row(g_ml_head), batch, seq)

    mem_h = _memnorm(mem.reshape(batch * mem_len, d), row(g_mem), tm=Tiles.mem_tm)
    kv = _proj(mem_h, w_mem_kv.astype(BF16), 0, 2 * D_MODEL, _identity, tm=Tiles.mem_tm,
               tn=Tiles.proj_tn, out_major=False)
    a_x = _xattn_branch(q_xa, z_xa, kv, mem_len, tm=Tiles.xattn_tm, seq=seq)

    merged = _merge(a_c, a_m, a_x, w_conv_out.astype(BF16), w_ml_out.astype(BF16),
                    w_xa_out.astype(BF16), gates, 0, tm=Tiles.merge_tm, tn=Tiles.merge_tn)
    out = _final(merged, w_out.astype(BF16), row(g_post), x2, tm=Tiles.final_tm)
    return out.reshape(batch, seq, d)
```

```python
import functools

import jax
import jax.numpy as jnp
from jax import lax
from jax.experimental import pallas as pl
from jax.experimental.pallas import tpu as pltpu

F32 = jnp.float32
BF16 = jnp.bfloat16

D_MODEL = 2048
N_HEADS = 4
HEAD_DIM = D_MODEL // N_HEADS
CONV_WIDTH = 31
QK_CONV_WIDTH = 4
EPS = 1e-6
NEG_BIG = -1e30

LANES = 128
SUBLANES = 8
VMEM_LIMIT = 56 * 1024 * 1024


class Tiles:
    prenorm_tm = 1024
    proj_tm, proj_tn = 1024, 2048
    glu_tm, glu_tn = 1024, 1024
    conv_tm = 1024
    xattn_tm = 1024
    mem_tm = 256
    merge_tm, merge_tn = 512, 1024
    final_tm = 512


def _sigmoid_t(y):
    return 0.5 * jnp.tanh(0.5 * y) + 0.5


def _silu_t(y):
    t = 0.5 * y
    return t * (jnp.tanh(t) + 1.0)


def _identity(y):
    return y


def _params(*sem):
    return pltpu.CompilerParams(dimension_semantics=sem, vmem_limit_bytes=VMEM_LIMIT)


def _prenorm_kernel(x_ref, g_ref, whl_ref, bif_ref, h_ref, gif_ref, gift_ref):
    xf = x_ref[...]
    y = xf * lax.rsqrt(jnp.mean(xf * xf, axis=-1, keepdims=True) + EPS) * g_ref[...]
    hi = y.astype(BF16)
    h_ref[...] = hi
    lo = (y - hi.astype(F32)).astype(BF16)
    hh = jnp.dot(hi, whl_ref[...], preferred_element_type=F32)
    lh = jnp.dot(lo, whl_ref[:, 0:LANES], preferred_element_type=F32)
    gif = hh[:, 0:LANES] + (hh[:, LANES:] + lh) + bif_ref[...]
    gif_ref[...] = gif
    gift_ref[0] = gif.T[0:SUBLANES, :]


def _prenorm(x2, g, w_if_hl, bif, tm, batch):
    n, d = x2.shape
    tiles_per_seq = n // batch // tm
    return pl.pallas_call(
        _prenorm_kernel,
        out_shape=(jax.ShapeDtypeStruct((n, d), BF16), jax.ShapeDtypeStruct((n, LANES), F32),
                   jax.ShapeDtypeStruct((batch, SUBLANES, n // batch), F32)),
        grid=(n // tm,),
        in_specs=[pl.BlockSpec((tm, d), lambda i: (i, 0)),
                  pl.BlockSpec((1, d), lambda i: (0, 0)),
                  pl.BlockSpec((d, 2 * LANES), lambda i: (0, 0)),
                  pl.BlockSpec((1, LANES), lambda i: (0, 0))],
        out_specs=(pl.BlockSpec((tm, d), lambda i: (i, 0)),
                   pl.BlockSpec((tm, LANES), lambda i: (i, 0)),
                   pl.BlockSpec((1, SUBLANES, tm),
                                lambda i: (i // tiles_per_seq, 0, i % tiles_per_seq))),
        compiler_params=_params("parallel"),
        name="prenorm",
    )(x2, g, w_if_hl, bif)


def _memnorm_kernel(x_ref, g_ref, h_ref):
    xf = x_ref[...]
    y = xf * lax.rsqrt(jnp.mean(xf * xf, axis=-1, keepdims=True) + EPS) * g_ref[...]
    h_ref[...] = y.astype(BF16)


def _memnorm(x2, g, tm):
    n, d = x2.shape
    return pl.pallas_call(
        _memnorm_kernel,
        out_shape=jax.ShapeDtypeStruct((n, d), BF16),
        grid=(n // tm,),
        in_specs=[pl.BlockSpec((tm, d), lambda i: (i, 0)),
                  pl.BlockSpec((1, d), lambda i: (0, 0))],
        out_specs=pl.BlockSpec((tm, d), lambda i: (i, 0)),
        compiler_params=_params("parallel"),
        name="memnorm",
    )(x2, g)


PROJ_SUB = 512


def _dot_t(a, w_t):
    return lax.dot_general(a, w_t, (((1,), (1,)), ((), ())), preferred_element_type=F32)


def _proj_kernel(h_ref, w_ref, o_ref, *, act, out_major):
    hh = h_ref[...]
    for c0 in range(0, o_ref.shape[1], PROJ_SUB):
        sl = slice(c0, c0 + PROJ_SUB)
        y = (_dot_t(hh, w_ref[sl, :]) if out_major
             else jnp.dot(hh, w_ref[:, sl], preferred_element_type=F32))
        o_ref[:, sl] = act(y).astype(o_ref.dtype)


def _proj(h, w, start, ncols, act, tm, tn, out_major=True):
    n, d = h.shape
    off = start // tn
    wspec = (pl.BlockSpec((tn, d), lambda j, i: (j + off, 0)) if out_major
             else pl.BlockSpec((d, tn), lambda j, i: (0, j + off)))
    return pl.pallas_call(
        functools.partial(_proj_kernel, act=act, out_major=out_major),
        out_shape=jax.ShapeDtypeStruct((n, ncols), BF16),
        grid=(ncols // tn, n // tm),
        in_specs=[pl.BlockSpec((tm, d), lambda j, i: (i, 0)), wspec],
        out_specs=pl.BlockSpec((tm, tn), lambda j, i: (i, j)),
        compiler_params=_params("parallel", "parallel"),
        name="proj",
    )(h, w)


def _glu_kernel(h_ref, wa_ref, wb_ref, o_ref):
    hh = h_ref[...]
    a = _dot_t(hh, wa_ref[...])
    b = _dot_t(hh, wb_ref[...])
    o_ref[...] = (a * _sigmoid_t(b)).astype(o_ref.dtype)


def _glu(h, w_t, tm, tn):
    n, d = h.shape
    ncol = D_MODEL // tn
    return pl.pallas_call(
        _glu_kernel,
        out_shape=jax.ShapeDtypeStruct((n, D_MODEL), BF16),
        grid=(ncol, n // tm),
        in_specs=[pl.BlockSpec((tm, d), lambda j, i: (i, 0)),
                  pl.BlockSpec((tn, d), lambda j, i: (j, 0)),
                  pl.BlockSpec((tn, d), lambda j, i: (j + ncol, 0))],
        out_specs=pl.BlockSpec((tm, tn), lambda j, i: (i, j)),
        compiler_params=_params("parallel", "parallel"),
        name="proj_glu",
    )(h, w_t, w_t)


CONV_HALO = 32
CONV_KB = 16
CONV_RB = 16
PACK = 2 * SUBLANES
ACC_PITCH = PACK + 1


def _conv_kernel(u_ref, prev_ref, z_ref, w_ref, bdw_ref, gln_ref, bln_ref, o_ref,
                 slab_ref, acc_ref, *, tm, tiles_per_seq):
    i = pl.program_id(0)
    first = (i % tiles_per_seq) == 0
    seg = tm // SUBLANES
    pitch = seg + 1
    base = CONV_HALO - (CONV_WIDTH - 1)
    half = CONV_KB // 2
    ncols = D_MODEL // LANES
    assert CONV_HALO < seg and seg % CONV_HALO == 0 and seg % CONV_KB == 0 and CONV_KB == CONV_RB

    def gap(nrow):
        return nrow + nrow // seg

    def build_col(c, carry):
        lanes = pl.ds(pl.multiple_of(c * LANES, LANES), LANES)
        slab_ref[c, 0:CONV_HALO, :] = jnp.where(first, 0.0, prev_ref[:, lanes].astype(F32))
        slab_ref[c, CONV_HALO:seg, :] = u_ref[0:seg - CONV_HALO, lanes].astype(F32)
        for q in range(1, SUBLANES):
            slab_ref[c, q * pitch:q * pitch + seg, :] = (
                u_ref[q * seg - CONV_HALO:(q + 1) * seg - CONV_HALO, lanes].astype(F32))
        slab_ref[c, SUBLANES * pitch:SUBLANES * pitch + CONV_HALO, :] = (
            u_ref[tm - CONV_HALO:tm, lanes].astype(F32))
        return carry

    lax.fori_loop(0, ncols, build_col, 0)

    def mac_col(c, carry):
        for k0 in range(0, seg, CONV_KB):
            pk = [jnp.concatenate(
                      [slab_ref[c, pl.ds(gap(k0 + base + m), SUBLANES, stride=pitch), :],
                       slab_ref[c, pl.ds(gap(k0 + base + m + half), SUBLANES, stride=pitch), :]],
                      axis=0).astype(BF16)
                  for m in range(half + CONV_WIDTH - 1)]
            for kk in range(half):
                acc = None
                for j in range(CONV_WIDTH):
                    t = w_ref[j, c].astype(F32) * pk[kk + j].astype(F32)
                    acc = t if acc is None else acc + t
                blk = (k0 // CONV_KB) * half + kk
                acc_ref[c, blk * ACC_PITCH:blk * ACC_PITCH + PACK, :] = acc
        return carry

    lax.fori_loop(0, ncols, mac_col, 0)

    blocks_per_seg = seg // CONV_RB

    def row_block(rb, carry):
        r = rb // blocks_per_seg
        k0 = (rb % blocks_per_seg) * CONV_RB
        start = (k0 // CONV_KB) * half * ACC_PITCH + r
        lo = acc_ref[:, pl.ds(start, SUBLANES, stride=ACC_PITCH), :]
        hi = acc_ref[:, pl.ds(start + SUBLANES, SUBLANES, stride=ACC_PITCH), :]
        y = jnp.concatenate([lo, hi], axis=1) + bdw_ref[...]
        mu = jnp.sum(jnp.sum(y, axis=0), axis=-1, keepdims=True) * (1.0 / D_MODEL)
        yc = y - mu
        var = jnp.sum(jnp.sum(yc * yc, axis=0), axis=-1, keepdims=True) * (1.0 / D_MODEL)
        yn = yc * lax.rsqrt(var + EPS) * gln_ref[...] + bln_ref[...]
        rows = pl.ds(pl.multiple_of(rb * CONV_RB, CONV_RB), CONV_RB)
        for c in range(ncols):
            sl = slice(c * LANES, (c + 1) * LANES)
            o_ref[rows, sl] = (_silu_t(yn[c]) * z_ref[rows, sl].astype(F32)).astype(o_ref.dtype)
        return carry

    lax.fori_loop(0, tm // CONV_RB, row_block, 0, unroll=8)


def _conv_branch(u, z, z_col, w_dw_b, b_dw, g_ln, b_ln, tm, seq):
    n, d = u.shape
    tiles_per_seq = seq // tm
    hb = tm // CONV_HALO
    vec = pl.BlockSpec((d // LANES, 1, LANES), lambda i: (0, 0, 0))
    return pl.pallas_call(
        functools.partial(_conv_kernel, tm=tm, tiles_per_seq=tiles_per_seq),
        out_shape=jax.ShapeDtypeStruct((n, d), BF16),
        grid=(n // tm,),
        in_specs=[pl.BlockSpec((tm, d), lambda i: (i, 0)),
                  pl.BlockSpec((CONV_HALO, d), lambda i: (jnp.maximum(i * hb - 1, 0), 0)),
                  pl.BlockSpec((tm, d), lambda i: (i, z_col)),
                  pl.BlockSpec((CONV_WIDTH, d // LANES, PACK, LANES), lambda i: (0, 0, 0, 0)),
                  vec, vec, vec],
        out_specs=pl.BlockSpec((tm, d), lambda i: (i, 0)),
        scratch_shapes=[pltpu.VMEM((d // LANES, tm + CONV_HALO + SUBLANES, LANES), F32),
                        pltpu.VMEM((d // LANES, tm // PACK * ACC_PITCH, LANES), F32)],
        compiler_params=_params("parallel"),
        name="conv_branch",
    )(u, u, z, w_dw_b, b_dw, g_ln, b_ln)


ML_CHUNK = 256
QK_HALO = 16


def _log_sigmoid(t):
    return jnp.minimum(t, 0.0) - jnp.log(1.0 + jnp.exp(-jnp.abs(t)))


def _mlstm_kernel(q_ref, k_ref, qprev_ref, kprev_ref, v_ref, cw_ref, shift_ref, gc_ref, gr_ref,
                  o_ref, z_ref, gh_ref, out_ref, c_ref, n_ref, m_ref):
    first = pl.program_id(1) == 0
    L = ML_CHUNK

    @pl.when(first)
    def _():
        c_ref[...] = jnp.zeros_like(c_ref)
        n_ref[...] = jnp.zeros_like(n_ref)
        m_ref[...] = jnp.zeros_like(m_ref)

    def conv4(pre_ref, prev_ref, a):
        pre = pre_ref[...]
        w = cw_ref[a]
        acc = w[QK_CONV_WIDTH - 1:QK_CONV_WIDTH, :] * pre.astype(F32)
        for s in range(1, QK_CONV_WIDTH):
            sh = jnp.dot(shift_ref[s - 1], pre, preferred_element_type=F32)
            acc = acc + w[QK_CONV_WIDTH - 1 - s:QK_CONV_WIDTH - s, :] * sh
        tail = jnp.where(first, 0.0, prev_ref[QK_HALO - SUBLANES:, :].astype(F32))
        rid = lax.broadcasted_iota(jnp.int32, tail.shape, 0)
        fix = jnp.zeros_like(tail)
        for s in range(1, QK_CONV_WIDTH):
            rolled = pltpu.roll(tail, s, axis=0)
            fix = fix + w[QK_CONV_WIDTH - 1 - s:QK_CONV_WIDTH - s, :] * jnp.where(rid < s, rolled, 0.0)
        acc = jnp.concatenate([acc[0:SUBLANES, :] + fix, acc[SUBLANES:, :]], axis=0)
        return _silu_t(acc)

    q_all = conv4(q_ref, qprev_ref, 0)
    k_all = conv4(k_ref, kprev_ref, 1) * (HEAD_DIM ** -0.5)

    ri = lax.broadcasted_iota(jnp.int32, (L, L), 0)
    ci = lax.broadcasted_iota(jnp.int32, (L, L), 1)
    causal = ri >= ci
    upper = ri <= ci
    gcol = gc_ref[...]
    lane = lax.broadcasted_iota(jnp.int32, gcol.shape, 1)
    grow = gr_ref[0]
    sub = lax.broadcasted_iota(jnp.int32, grow.shape, 0)

    for h in range(N_HEADS):
        hs = slice(h * HEAD_DIM, (h + 1) * HEAD_DIM)
        q = q_all[:, hs]
        k = k_all[:, hs]
        v = v_ref[:, hs]
        qb = q.astype(BF16)

        li_col = jnp.sum(jnp.where(lane == h, gcol, 0.0), axis=-1, keepdims=True)
        f_col = jnp.sum(jnp.where(lane == h + N_HEADS, gcol, 0.0), axis=-1, keepdims=True)
        li_row = jnp.sum(jnp.where(sub == h, grow, 0.0), axis=0, keepdims=True)
        f_row = jnp.sum(jnp.where(sub == h + N_HEADS, grow, 0.0), axis=0, keepdims=True)
        lf_col = _log_sigmoid(f_col)
        lf_row = _log_sigmoid(f_row)
        b_col = jnp.sum(jnp.where(causal, lf_row, 0.0), axis=-1, keepdims=True)
        b_row = jnp.sum(jnp.where(upper, lf_col, 0.0), axis=0, keepdims=True)

        m_prev = m_ref[h, 0:1, 0:1]
        d = jnp.where(causal, b_col - b_row + li_row, NEG_BIG)
        inter = b_col + m_prev
        m_row = jnp.maximum(inter, jnp.max(d, axis=-1, keepdims=True))
        w_intra = jnp.exp(d - m_row)
        w_inter = jnp.exp(inter - m_row)

        s = lax.dot_general(qb, k.astype(BF16), (((1,), (1,)), ((), ())),
                            preferred_element_type=F32) * w_intra
        num = (jnp.dot(s.astype(BF16), v, preferred_element_type=F32)
               + w_inter * jnp.dot(qb, c_ref[h].astype(BF16), preferred_element_type=F32))
        den = (jnp.sum(s, axis=-1, keepdims=True)
               + w_inter * jnp.sum(q * n_ref[h], axis=-1, keepdims=True))
        hval = num / jnp.maximum(jnp.abs(den), jnp.exp(-m_row))

        b_last = b_col[L - 1:L, :]
        g_row = b_last - b_row + li_row
        g_col = b_last - b_col + li_col
        m_new = jnp.maximum(b_last + m_prev, jnp.max(g_row, axis=-1, keepdims=True))
        decay = jnp.exp(b_last + m_prev - m_new)
        kw = k * jnp.exp(g_col - m_new)
        c_ref[h] = decay * c_ref[h] + lax.dot_general(
            kw.astype(BF16), v, (((0,), (0,)), ((), ())), preferred_element_type=F32)
        n_ref[h] = decay * n_ref[h] + jnp.sum(kw, axis=0, keepdims=True)
        m_ref[h] = jnp.broadcast_to(m_new, m_ref.shape[1:])

        hm = o_ref[:, hs].astype(F32) * hval
        hm = hm * lax.rsqrt(jnp.mean(hm * hm, axis=-1, keepdims=True) + EPS) * gh_ref[:, hs]
        out_ref[:, hs] = (hm * z_ref[:, hs].astype(F32)).astype(out_ref.dtype)


def _mlstm_branch(qkv, o_gate, z_gate, cw, shifts, gif, gif_t, g_head, batch, seq):
    n = qkv.shape[0]
    L = ML_CHUNK
    nc = seq // L
    hb = L // QK_HALO

    def tok(col_group):
        return pl.BlockSpec((L, D_MODEL), lambda b, c: (b * nc + c, col_group))

    def prev(col_group):
        return pl.BlockSpec((QK_HALO, D_MODEL),
                            lambda b, c: (jnp.maximum((b * nc + c) * hb - 1, 0), col_group))

    return pl.pallas_call(
        _mlstm_kernel,
        out_shape=jax.ShapeDtypeStruct((n, D_MODEL), BF16),
        grid=(batch, nc),
        in_specs=[tok(0), tok(1), prev(0), prev(1), tok(2),
                  pl.BlockSpec((2, QK_CONV_WIDTH, D_MODEL), lambda b, c: (0, 0, 0)),
                  pl.BlockSpec((QK_CONV_WIDTH - 1, L, L), lambda b, c: (0, 0, 0)),
                  pl.BlockSpec((L, LANES), lambda b, c: (b * nc + c, 0)),
                  pl.BlockSpec((1, SUBLANES, L), lambda b, c: (b, 0, c)),
                  tok(0), tok(0),
                  pl.BlockSpec((1, D_MODEL), lambda b, c: (0, 0))],
        out_specs=pl.BlockSpec((L, D_MODEL), lambda b, c: (b * nc + c, 0)),
        scratch_shapes=[pltpu.VMEM((N_HEADS, HEAD_DIM, HEAD_DIM), F32),
                        pltpu.VMEM((N_HEADS, 1, HEAD_DIM), F32),
                        pltpu.VMEM((N_HEADS, SUBLANES, LANES), F32)],
        compiler_params=_params("parallel", "arbitrary"),
        name="mlstm",
    )(qkv, qkv, qkv, qkv, qkv, cw, shifts, gif, gif_t, o_gate, z_gate, g_head)


def _xattn_kernel(q_ref, k_ref, v_ref, z_ref, o_ref):
    scale = HEAD_DIM ** -0.5
    for h in range(N_HEADS):
        sl = slice(h * HEAD_DIM, (h + 1) * HEAD_DIM)
        s = lax.dot_general(q_ref[:, sl], k_ref[:, sl], (((1,), (1,)), ((), ())),
                            preferred_element_type=F32) * scale
        e = jnp.exp(s - jnp.max(s, axis=-1, keepdims=True))
        p = e / jnp.sum(e, axis=-1, keepdims=True)
        o = jnp.dot(p.astype(BF16), v_ref[:, sl], preferred_element_type=F32)
        o_ref[:, sl] = (o * z_ref[:, sl].astype(F32)).astype(o_ref.dtype)


def _xattn_branch(q, z_gate, kv, mem_len, tm, seq):
    n = q.shape[0]
    tiles_per_seq = seq // tm
    return pl.pallas_call(
        _xattn_kernel,
        out_shape=jax.ShapeDtypeStruct((n, D_MODEL), BF16),
        grid=(n // tm,),
        in_specs=[pl.BlockSpec((tm, D_MODEL), lambda i: (i, 0)),
                  pl.BlockSpec((mem_len, D_MODEL), lambda i: (i // tiles_per_seq, 0)),
                  pl.BlockSpec((mem_len, D_MODEL), lambda i: (i // tiles_per_seq, 1)),
                  pl.BlockSpec((tm, D_MODEL), lambda i: (i, 0))],
        out_specs=pl.BlockSpec((tm, D_MODEL), lambda i: (i, 0)),
        compiler_params=_params("parallel"),
        name="xattn",
    )(q, kv, kv, z_gate)


def _merge_kernel(ac_ref, am_ref, ax_ref, wc_ref, wm_ref, wx_ref, gc_ref, gm_ref, gx_ref, o_ref):
    acc = gc_ref[...].astype(F32) * jnp.dot(ac_ref[...], wc_ref[...], preferred_element_type=F32)
    acc = acc + gm_ref[...].astype(F32) * jnp.dot(am_ref[...], wm_ref[...], preferred_element_type=F32)
    acc = acc + gx_ref[...].astype(F32) * jnp.dot(ax_ref[...], wx_ref[...], preferred_element_type=F32)
    o_ref[...] = acc.astype(o_ref.dtype)


def _merge(a_c, a_m, a_x, w_c, w_m, w_x, gates, gate_col, tm, tn):
    n, d = a_c.shape
    ncol = D_MODEL // tn
    act = pl.BlockSpec((tm, d), lambda j, i: (i, 0))
    wsp = pl.BlockSpec((d, tn), lambda j, i: (0, j))

    def gate(k):
        return pl.BlockSpec((tm, tn), lambda j, i: (i, (gate_col + k) * ncol + j))

    return pl.pallas_call(
        _merge_kernel,
        out_shape=jax.ShapeDtypeStruct((n, D_MODEL), BF16),
        grid=(ncol, n // tm),
        in_specs=[act, act, act, wsp, wsp, wsp, gate(0), gate(1), gate(2)],
        out_specs=pl.BlockSpec((tm, tn), lambda j, i: (i, j)),
        compiler_params=_params("parallel", "parallel"),
        name="merge",
    )(a_c, a_m, a_x, w_c, w_m, w_x, gates, gates, gates)


def _final_kernel(m_ref, w_ref, g_ref, x_ref, o_ref):
    y = jnp.dot(m_ref[...], w_ref[...], preferred_element_type=F32)
    y = y * lax.rsqrt(jnp.mean(y * y, axis=-1, keepdims=True) + EPS) * g_ref[...]
    o_ref[...] = x_ref[...] + y


def _final(merged, w_out, g_post, x2, tm):
    n, d = x2.shape
    return pl.pallas_call(
        _final_kernel,
        out_shape=jax.ShapeDtypeStruct((n, d), F32),
        grid=(n // tm,),
        in_specs=[pl.BlockSpec((tm, d), lambda i: (i, 0)),
                  pl.BlockSpec((d, d), lambda i: (0, 0)),
                  pl.BlockSpec((1, d), lambda i: (0, 0)),
                  pl.BlockSpec((tm, d), lambda i: (i, 0))],
        out_specs=pl.BlockSpec((tm, d), lambda i: (i, 0)),
        compiler_params=_params("parallel"),
        name="final",
    )(merged, w_out, g_post, x2)


def kernel(x, mem, g_pre, w_in, b_if, w_qk_conv, w_dw, b_dw, g_ln, b_ln, w_conv_out, g_ml_head,
           w_ml_out, g_mem, w_mem_kv, w_xa_out, w_out, g_post):
    batch, seq, d = x.shape
    mem_len = mem.shape[1]
    n = batch * seq
    assert d == D_MODEL and seq % ML_CHUNK == 0
    for t in (Tiles.prenorm_tm, Tiles.conv_tm, Tiles.xattn_tm):
        assert seq % t == 0, "these tiles must not straddle two sequences"
    for t in (Tiles.proj_tm, Tiles.glu_tm, Tiles.merge_tm, Tiles.final_tm):
        assert n % t == 0

    nif = 2 * N_HEADS
    if0 = 8 * D_MODEL
    w_in_t = w_in.T
    w_head = w_in_t.astype(BF16)
    w_tail = w_head[if0 + nif:]

    w_if = jnp.pad(w_in[:, 8 * D_MODEL:8 * D_MODEL + nif], ((0, 0), (0, LANES - nif)))
    w_if_hi = w_if.astype(BF16)
    w_if_lo = (w_if - w_if_hi.astype(F32)).astype(BF16)
    w_if_hl = jnp.concatenate([w_if_hi, w_if_lo], axis=1)
    b_if_pad = jnp.pad(b_if, (0, LANES - nif)).reshape(1, LANES)

    def row(vec):
        return vec.reshape(1, -1).astype(F32)

    def lane_cols(vec):
        return vec.reshape(d // LANES, 1, LANES).astype(F32)

    x2 = x.reshape(n, d)
    h, gif, gif_t = _prenorm(x2, row(g_pre), w_if_hl, b_if_pad, tm=Tiles.prenorm_tm, batch=batch)

    def proj(w_t, group, ngroups, act):
        return _proj(h, w_t, group * D_MODEL, ngroups * D_MODEL, act,
                     tm=Tiles.proj_tm, tn=Tiles.proj_tn)

    u = _glu(h, w_head, tm=Tiles.glu_tm, tn=Tiles.glu_tn)
    z_conv = proj(w_head, 2, 1, _silu_t)
    qkv = proj(w_head, 3, 3, _identity)
    o_gate = proj(w_head, 6, 1, _sigmoid_t)
    z_ml = proj(w_head, 7, 1, _silu_t)
    q_xa = proj(w_tail, 0, 1, _identity)
    z_xa = proj(w_tail, 1, 1, _silu_t)
    gates = proj(w_tail, 2, 3, _sigmoid_t)

    w_dw_b = jnp.broadcast_to(w_dw.astype(BF16).reshape(CONV_WIDTH, d // LANES, 1, LANES),
                              (CONV_WIDTH, d // LANES, PACK, LANES))
    a_c = _conv_branch(u, z_conv, 0, w_dw_b, lane_cols(b_dw), lane_cols(g_ln),
                       lane_cols(b_ln), tm=Tiles.conv_tm, seq=seq)

    cw = w_qk_conv.astype(F32).reshape(QK_CONV_WIDTH, 2, d).transpose(1, 0, 2)
    t_idx = jnp.arange(ML_CHUNK)
    shifts = jnp.stack([(t_idx[:, None] - s == t_idx[None, :]).astype(BF16)
                        for s in range(1, QK_CONV_WIDTH)])
    a_m = _mlstm_branch(qkv, o_gate, z_ml, cw, shifts, gif, gif_t, row(g_ml_head), batch, seq)

    mem_h = _memnorm(mem.reshape(batch * mem_len, d), row(g_mem), tm=Tiles.mem_tm)
    kv = _proj(mem_h, w_mem_kv.astype(BF16), 0, 2 * D_MODEL, _identity, tm=Tiles.mem_tm,
               tn=Tiles.proj_tn, out_major=False)
    a_x = _xattn_branch(q_xa, z_xa, kv, mem_len, tm=Tiles.xattn_tm, seq=seq)

    merged = _merge(a_c, a_m, a_x, w_conv_out.astype(BF16), w_ml_out.astype(BF16),
                    w_xa_out.astype(BF16), gates, 0, tm=Tiles.merge_tm, tn=Tiles.merge_tn)
    out = _final(merged, w_out.astype(BF16), row(g_post), x2, tm=Tiles.final_tm)
    return out.reshape(batch, seq, d)
```
